```python
import math
import jax, jax.numpy as jnp
from jax import lax
import numpy as np

D_MODEL = 1024
BATCH = 32
SEQ = 256
DEPTH = 4
DEC_BATCH = 2
DEC_SEQ = 1024
PAST_LEN = 256

GRID_W = 64
N_MIXERS = 4
N_MOD = 9
EPS = 1e-6
FFN_HIDDEN = 2816

M_INNER = 2 * D_MODEL
M_HEADDIM = 64
M_HEADS = M_INNER // M_HEADDIM
M_GROUPS = 4
M_STATE = 128
M_CONV = 3
M_CHUNK = 128
M_CONV_CH = M_INNER + 2 * M_GROUPS * M_STATE
M_IN = M_INNER + M_CONV_CH + 2 * M_HEADS

G_CHUNK = 128
G_INNER = 2 * D_MODEL
G_HEADS = 8

F_GROUPS = 4

A_HEADS = 16
A_KV = 4
A_HD = 64
A_BLOCK = 128
ROPE_THETA = 10000.0

F32 = jnp.float32

kernel_name = 'hybrid_diffusion_macaron_ssd_gmlp_fnet_gqa_step'


def _layers_of(kind):
    return max(0, (DEPTH - kind + N_MIXERS - 1) // N_MIXERS)


def rms_norm(x, g):
    xf = x.astype(F32)
    y = xf * lax.rsqrt(jnp.mean(xf * xf, axis=-1, keepdims=True) + EPS)
    return (y * g.astype(F32)).astype(x.dtype)


def modulate(x, shift, scale):
    return x * (1 + scale[:, None]) + shift[:, None]


def adaln(cond, w, b):
    m = jax.nn.silu(cond.astype(F32)).astype(w.dtype) @ w + b
    return m.reshape(cond.shape[0], N_MOD, D_MODEL)


def swiglu(h, w_in, w_out):
    gate, up = jnp.split(h @ w_in, 2, axis=-1)
    return (jax.nn.silu(gate) * up) @ w_out


def macaron_half(x, g_norm, shift, scale, gate, w_in, w_out):
    h = modulate(rms_norm(x, g_norm), shift, scale)
    return x + 0.5 * gate[:, None] * swiglu(h, w_in, w_out)


def dw_conv_centred(x, w, b):
    k = w.shape[0]
    y = lax.conv_general_dilated(x, w[:, None, :].astype(x.dtype), window_strides=(1,),
                                 padding=[(k // 2, k // 2)],
                                 dimension_numbers=('NWC', 'WIO', 'NWC'),
                                 feature_group_count=x.shape[-1])
    return y + b


def ssd_scan(x, dt, a, bm, cm, h0):
    b, l, nh, hp = x.shape
    g, n = bm.shape[2], bm.shape[3]
    r = nh // g
    q = M_CHUNK
    nc = l // q
    xc = x.reshape(b, nc, q, g, r, hp)
    dtc = dt.reshape(b, nc, q, g, r)
    bc = bm.reshape(b, nc, q, g, n)
    cc = cm.reshape(b, nc, q, g, n)
    acum = jnp.cumsum(dtc * a.reshape(g, r), axis=2)
    seg = acum[:, :, :, None] - acum[:, :, None, :]
    lower = jnp.tril(jnp.ones((q, q), bool))[:, :, None, None]
    decay = jnp.exp(jnp.where(lower, seg, -jnp.inf))
    xdt = xc * dtc[..., None]
    cb = jnp.einsum('bcign,bcjgn->bcijg', cc, bc)
    y_diag = jnp.einsum('bcijg,bcijgr,bcjgrp->bcigrp', cb, decay, xdt)
    to_end = jnp.exp(acum[:, :, -1:] - acum)
    chunk_states = jnp.einsum('bcjgn,bcjgr,bcjgrp->bcgrpn', bc, to_end, xdt)
    chunk_decay = jnp.exp(acum[:, :, -1])

    def step(h_prev, inp):
        s, dcy = inp
        return h_prev * dcy[..., None, None] + s, h_prev

    h_final, h_in = lax.scan(step, h0.reshape(b, g, r, hp, n),
                             (jnp.moveaxis(chunk_states, 1, 0), jnp.moveaxis(chunk_decay, 1, 0)))
    h_in = jnp.moveaxis(h_in, 0, 1)
    y_off = jnp.einsum('bcign,bcgrpn,bcigr->bcigrp', cc, h_in, jnp.exp(acum))
    y = (y_diag + y_off).reshape(b, l, nh, hp)
    return y, h_final.reshape(b, nh, hp, n)


def mamba_mixer(h, h0, w_in, conv_w, conv_b, dt_bias, a_log, d_skip, norm_g, w_out):
    b, l, _ = h.shape
    proj = h @ w_in
    z = proj[..., :M_INNER]
    xbc = proj[..., M_INNER:M_INNER + M_CONV_CH]
    dt_raw = proj[..., M_INNER + M_CONV_CH:]
    xbc = jax.nn.silu(dw_conv_centred(xbc, conv_w, conv_b))
    gn = M_GROUPS * M_STATE
    xs = xbc[..., :M_INNER].reshape(b, l, M_HEADS, M_HEADDIM).astype(F32)
    bm = xbc[..., M_INNER:M_INNER + gn].reshape(b, l, M_GROUPS, M_STATE).astype(F32)
    cm = xbc[..., M_INNER + gn:].reshape(b, l, M_GROUPS, M_STATE).astype(F32)
    dt = jax.nn.softplus(dt_raw.reshape(b, l, 2, M_HEADS).astype(F32) + dt_bias.astype(F32))
    a = -jnp.exp(a_log.astype(F32))
    h0 = h0.astype(F32)
    rev = lambda t: jnp.flip(t, axis=1)
    y_f, s_f = ssd_scan(xs, dt[:, :, 0], a[0], bm, cm, h0[:, 0])
    y_b, s_b = ssd_scan(rev(xs), rev(dt[:, :, 1]), a[1], rev(bm), rev(cm), h0[:, 1])
    y = y_f + rev(y_b) + xs * d_skip.astype(F32)[:, None]
    y = y.reshape(b, l, M_INNER) * jax.nn.silu(z.astype(F32))
    yg = y.reshape(b, l, M_GROUPS, M_INNER // M_GROUPS)
    yg = yg * lax.rsqrt(jnp.mean(yg * yg, axis=-1, keepdims=True) + EPS)
    y = (yg.reshape(b, l, M_INNER) * norm_g.astype(F32)).astype(h.dtype)
    return y @ w_out, jnp.stack([s_f, s_b], axis=1).astype(h.dtype)


def gmlp_mixer(h, w_in, b_in, norm_g, w_s, b_s, w_out):
    b, l, _ = h.shape
    u, v = jnp.split(jax.nn.gelu(h @ w_in + b_in), 2, axis=-1)
    v = rms_norm(v, norm_g)
    vc = v.reshape(b, l // G_CHUNK, G_CHUNK, G_HEADS, G_INNER // G_HEADS)
    sv = jnp.einsum('hij,bcjhd->bcihd', w_s, vc) + b_s.T[:, :, None]
    return (u * sv.reshape(b, l, G_INNER)) @ w_out


def fourier_mixer(h, w_out, b_out):
    b, l, d = h.shape
    hg = h.astype(F32).reshape(b, l, F_GROUPS, d // F_GROUPS).transpose(0, 2, 1, 3)
    f = jnp.fft.fft2(hg, norm='ortho').real
    f = f.transpose(0, 2, 1, 3).reshape(b, l, d).astype(h.dtype)
    return f @ w_out + b_out


def attn_qkv(h, w_qkv, q_g, k_g):
    b, l, _ = h.shape
    qkv = h @ w_qkv
    q = qkv[..., :A_HEADS * A_HD].reshape(b, l, A_HEADS, A_HD)
    k = qkv[..., A_HEADS * A_HD:(A_HEADS + A_KV) * A_HD].reshape(b, l, A_KV, A_HD)
    v = qkv[..., (A_HEADS + A_KV) * A_HD:].reshape(b, l, A_KV, A_HD)
    return rms_norm(q, q_g), rms_norm(k, k_g), v


def _rope_1d(x, pos):
    half = x.shape[-1] // 2
    inv = ROPE_THETA ** (-jnp.arange(half, dtype=F32) / half)
    ang = pos.astype(F32)[:, None] * inv[None, :]
    cos = jnp.cos(ang)[None, :, None, :]
    sin = jnp.sin(ang)[None, :, None, :]
    x1, x2 = x[..., :half], x[..., half:]
    return jnp.concatenate([x1 * cos - x2 * sin, x1 * sin + x2 * cos], axis=-1)


def axial_rope(x):
    l = x.shape[1]
    rows = l // GRID_W
    row = jnp.repeat(jnp.arange(rows), GRID_W)
    col = jnp.tile(jnp.arange(GRID_W), rows)
    xf = x.astype(F32)
    hd2 = x.shape[-1] // 2
    out = jnp.concatenate([_rope_1d(xf[..., :hd2], row), _rope_1d(xf[..., hd2:], col)], axis=-1)
    return out.astype(x.dtype)


def gqa_attend(q, k, v):
    b, lq, nh, hd = q.shape
    grp = nh // A_KV
    nb = lq // A_BLOCK
    qb = q.reshape(b, nb, A_BLOCK, A_KV, grp, hd).transpose(1, 0, 2, 3, 4, 5)
    kf = k.astype(F32)
    vf = v.astype(F32)
    scale = hd ** -0.5

    def one_block(qblk):
        s = jnp.einsum('bqkgd,bskd->bkgqs', qblk.astype(F32), kf) * scale
        p = jax.nn.softmax(s, axis=-1)
        return jnp.einsum('bkgqs,bskd->bqkgd', p, vf).astype(q.dtype)

    o = lax.map(one_block, qb)
    return o.transpose(1, 0, 2, 3, 4, 5).reshape(b, lq, nh * hd)


def setup_inputs(seed: int = 0) -> dict:
    key = jax.random.key(seed)
    keys = iter(jax.random.split(key, 48))

    def nrm(shape, scale):
        return jax.random.normal(next(keys), shape, F32) * scale

    def gain(shape):
        return 1.0 + nrm(shape, 0.02)

    na, nb, nc, nd = (_layers_of(k) for k in range(N_MIXERS))
    D = D_MODEL
    dt0 = jnp.exp(jax.random.uniform(next(keys), (na, 2, M_HEADS), F32, math.log(1e-3), math.log(1e-1)))
    a_init = jax.random.uniform(next(keys), (na, 2, M_HEADS), F32, 1.0, 16.0)
    return {
        'x_prompt': nrm((BATCH, SEQ, D), 1.0),
        'x_sample': nrm((DEC_BATCH, DEC_SEQ, D), 1.0),
        'state_ssm': nrm((DEC_BATCH, na, 2, M_HEADS, M_HEADDIM, M_STATE), 0.5),
        'cache_k': nrm((DEC_BATCH, nd, PAST_LEN, A_KV, A_HD), 1.0),
        'cache_v': nrm((DEC_BATCH, nd, PAST_LEN, A_KV, A_HD), 1.0),
        'c': nrm((DEC_BATCH, D), 1.0),
        'c_ctx': nrm((D,), 1.0),
        'ln_g': gain((DEPTH, 3, D)),
        'ada_w': nrm((DEPTH, D, N_MOD * D), 0.5 * D ** -0.5),
        'ada_b': nrm((DEPTH, N_MOD * D), 0.02),
        'ff1_w_in': nrm((DEPTH, D, 2 * FFN_HIDDEN), D ** -0.5),
        'ff1_w_out': nrm((DEPTH, FFN_HIDDEN, D), FFN_HIDDEN ** -0.5),
        'ff2_w_in': nrm((DEPTH, D, 2 * FFN_HIDDEN), D ** -0.5),
        'ff2_w_out': nrm((DEPTH, FFN_HIDDEN, D), FFN_HIDDEN ** -0.5),
        'm_w_in': nrm((na, D, M_IN), D ** -0.5),
        'm_conv_w': nrm((na, M_CONV, M_CONV_CH), M_CONV ** -0.5),
        'm_conv_b': nrm((na, M_CONV_CH), 0.02),
        'm_dt_bias': dt0 + jnp.log(-jnp.expm1(-dt0)),
        'm_a_log': jnp.log(a_init),
        'm_d': gain((na, M_HEADS)),
        'm_norm_g': gain((na, M_INNER)),
        'm_w_out': nrm((na, M_INNER, D), M_INNER ** -0.5),
        'g_w_in': nrm((nb, D, 2 * G_INNER), D ** -0.5),
        'g_b_in': nrm((nb, 2 * G_INNER), 0.02),
        'g_norm_g': gain((nb, G_INNER)),
        'g_w_s': nrm((nb, G_HEADS, G_CHUNK, G_CHUNK), G_CHUNK ** -0.5),
        'g_b_s': gain((nb, G_HEADS, G_CHUNK)),
        'g_w_out': nrm((nb, G_INNER, D), G_INNER ** -0.5),
        'f_w_out': nrm((nc, D, D), D ** -0.5),
        'f_b_out': nrm((nc, D), 0.02),
        'a_w_qkv': nrm((nd, D, (A_HEADS + 2 * A_KV) * A_HD), D ** -0.5),
        'a_q_norm': gain((nd, A_HD)),
        'a_k_norm': gain((nd, A_HD)),
        'a_w_o': nrm((nd, A_HEADS * A_HD, D), (A_HEADS * A_HD) ** -0.5),
    }


def reference(x_prompt, x_sample, state_ssm, cache_k, cache_v, c, c_ctx,
              ln_g, ada_w, ada_b, ff1_w_in, ff1_w_out, ff2_w_in, ff2_w_out,
              m_w_in, m_conv_w, m_conv_b, m_dt_bias, m_a_log, m_d, m_norm_g, m_w_out,
              g_w_in, g_b_in, g_norm_g, g_w_s, g_b_s, g_w_out,
              f_w_out, f_b_out,
              a_w_qkv, a_q_norm, a_k_norm, a_w_o):
    xp, xs = x_prompt, x_sample
    new_ssm, new_k, new_v = [], [], []
    for i in range(DEPTH):
        kind, j = i % N_MIXERS, i // N_MIXERS
        mc = adaln(c_ctx[None], ada_w[i], ada_b[i])
        msm = adaln(c, ada_w[i], ada_b[i])
        xp = macaron_half(xp, ln_g[i, 0], mc[:, 0], mc[:, 1], mc[:, 2], ff1_w_in[i], ff1_w_out[i])
        xs = macaron_half(xs, ln_g[i, 0], msm[:, 0], msm[:, 1], msm[:, 2], ff1_w_in[i], ff1_w_out[i])
        hp = modulate(rms_norm(xp, ln_g[i, 1]), mc[:, 3], mc[:, 4])
        hs = modulate(rms_norm(xs, ln_g[i, 1]), msm[:, 3], msm[:, 4])
        if kind == 0:
            zero_state = jnp.zeros((xp.shape[0], 2, M_HEADS, M_HEADDIM, M_STATE), xp.dtype)
            mix_p, st = mamba_mixer(hp, zero_state, m_w_in[j], m_conv_w[j], m_conv_b[j], m_dt_bias[j],
                                    m_a_log[j], m_d[j], m_norm_g[j], m_w_out[j])
            mix_s, _ = mamba_mixer(hs, state_ssm[:, j], m_w_in[j], m_conv_w[j], m_conv_b[j], m_dt_bias[j],
                                   m_a_log[j], m_d[j], m_norm_g[j], m_w_out[j])
            new_ssm.append(st)
        elif kind == 1:
            mix_p = gmlp_mixer(hp, g_w_in[j], g_b_in[j], g_norm_g[j], g_w_s[j], g_b_s[j], g_w_out[j])
            mix_s = gmlp_mixer(hs, g_w_in[j], g_b_in[j], g_norm_g[j], g_w_s[j], g_b_s[j], g_w_out[j])
        elif kind == 2:
            mix_p = fourier_mixer(hp, f_w_out[j], f_b_out[j])
            mix_s = fourier_mixer(hs, f_w_out[j], f_b_out[j])
        else:
            qc, kc, vc = attn_qkv(hp, a_w_qkv[j], a_q_norm[j], a_k_norm[j])
            mix_p = gqa_attend(qc, kc, vc) @ a_w_o[j]
            new_k.append(kc)
            new_v.append(vc)
            ql, kl, vl = attn_qkv(hs, a_w_qkv[j], a_q_norm[j], a_k_norm[j])
            ql, kl = axial_rope(ql), axial_rope(kl)
            k_all = jnp.concatenate([cache_k[:, j].astype(kl.dtype), kl], axis=1)
            v_all = jnp.concatenate([cache_v[:, j].astype(vl.dtype), vl], axis=1)
            mix_s = gqa_attend(ql, k_all, v_all) @ a_w_o[j]
        xp = xp + mc[:, 5][:, None] * mix_p
        xs = xs + msm[:, 5][:, None] * mix_s
        xp = macaron_half(xp, ln_g[i, 2], mc[:, 6], mc[:, 7], mc[:, 8], ff2_w_in[i], ff2_w_out[i])
        xs = macaron_half(xs, ln_g[i, 2], msm[:, 6], msm[:, 7], msm[:, 8], ff2_w_in[i], ff2_w_out[i])
    return (xp, xs, jnp.stack(new_ssm, axis=1), jnp.stack(new_k, axis=1), jnp.stack(new_v, axis=1))
```

```python
import functools
import math

import numpy as np
import jax
import jax.numpy as jnp
from jax import lax
from jax.experimental import pallas as pl
from jax.experimental.pallas import tpu as pltpu

F32 = jnp.float32
BF16 = jnp.bfloat16

D_MODEL = 1024
BATCH = 32
SEQ = 256
DEPTH = 4
DEC_BATCH = 2
DEC_SEQ = 1024
PAST_LEN = 256
GRID_W = 64
N_MIXERS = 4
N_MOD = 9
EPS = 1e-6
FFN_HIDDEN = 2816

M_INNER = 2 * D_MODEL
M_HEADDIM = 64
M_HEADS = M_INNER // M_HEADDIM
M_GROUPS = 4
M_STATE = 128
M_CONV = 3
M_CHUNK = 128
M_CONV_CH = M_INNER + 2 * M_GROUPS * M_STATE
M_GROUP_CH = M_INNER // M_GROUPS
M_GROUP_HEADS = M_HEADS // M_GROUPS
M_HEAD_PAIRS = M_GROUP_HEADS // 2

G_CHUNK = 128
G_INNER = 2 * D_MODEL
G_HEADS = 8
G_HEAD_CH = G_INNER // G_HEADS

F_GROUPS = 4
F_GROUP_CH = D_MODEL // F_GROUPS

A_HEADS = 16
A_KV = 4
A_HD = 64
A_GRP = A_HEADS // A_KV
A_Q = A_HEADS * A_HD
A_KVW = A_KV * A_HD
ROPE_THETA = 10000.0

LANES = 128
COND_ROWS = 8
TP = BATCH * SEQ
TS = DEC_BATCH * DEC_SEQ
VMEM_LIMIT = 56 * 2**20


def _cparams(n_grid):
    return pltpu.CompilerParams(dimension_semantics=("arbitrary",) * n_grid,
                                vmem_limit_bytes=VMEM_LIMIT)


def _resident(block_shape, index_map):
    return pl.BlockSpec(block_shape, index_map, pipeline_mode=pl.Buffered(1))


def _dot(a, b):
    return jnp.dot(a, b, preferred_element_type=F32)


def _dot_nt(a, b):
    return lax.dot_general(a, b, (((1,), (1,)), ((), ())), preferred_element_type=F32)


def _silu(x):
    return x * jax.nn.sigmoid(x)


def _softplus(x):
    return jnp.maximum(x, 0.0) + jnp.log1p(jnp.exp(-jnp.abs(x)))


def _modnorm(x, g, shift, scale):
    y = x * lax.rsqrt(jnp.mean(x * x, axis=-1, keepdims=True) + EPS) * g
    return y * (1.0 + scale) + shift


def _split3(v):
    hi = v.astype(BF16)
    r1 = v - hi.astype(F32)
    mid = r1.astype(BF16)
    lo = (r1 - mid.astype(F32)).astype(BF16)
    return hi, mid, lo


def _split2(v):
    hi = v.astype(BF16)
    return hi, (v - hi.astype(F32)).astype(BF16)


def _adaln_kernel(c_ref, w_ref, b_ref, o_ref):
    s = _silu(c_ref[...]).astype(BF16)
    o_ref[...] = _dot(s, w_ref[...].astype(BF16)) + b_ref[...]


def _adaln(conds, ada_w, ada_b):
    tn = 2304
    n = N_MOD * D_MODEL
    out = pl.pallas_call(
        _adaln_kernel,
        grid=(DEPTH, n // tn),
        in_specs=[pl.BlockSpec((COND_ROWS, D_MODEL), lambda i, j: (0, 0)),
                  pl.BlockSpec((None, D_MODEL, tn), lambda i, j: (i, 0, j)),
                  pl.BlockSpec((None, 1, tn), lambda i, j: (i, 0, j))],
        out_specs=pl.BlockSpec((None, COND_ROWS, tn), lambda i, j: (i, 0, j)),
        out_shape=jax.ShapeDtypeStruct((DEPTH, COND_ROWS, n), F32),
        compiler_params=_cparams(2),
        name="adaln",
    )(conds, ada_w, ada_b.reshape(DEPTH, 1, n))
    return out.reshape(DEPTH, COND_ROWS, N_MOD, D_MODEL)


def _mod_spec(layer, pop, tm):
    if pop == 0:
        return pl.BlockSpec((None, None, N_MOD, D_MODEL), lambda t, *_: (layer, 0, 0, 0))
    return pl.BlockSpec((None, None, N_MOD, D_MODEL),
                        lambda t, *_: (layer, 1 + (t * tm) // DEC_SEQ, 0, 0))


def _ln_spec(layer):
    return pl.BlockSpec((None, 3, D_MODEL), lambda t, *_: (layer, 0, 0))


def _ffn_kernel(x_ref, mod_ref, g_ref, win_ref, wout_ref, o_ref, *, k0, gk):
    x = x_ref[...]
    h = _modnorm(x, g_ref[gk:gk + 1, :], mod_ref[k0:k0 + 1, :], mod_ref[k0 + 1:k0 + 2, :])
    gu = _dot(h.astype(BF16), win_ref[...])
    act = _silu(gu[:, :FFN_HIDDEN]) * gu[:, FFN_HIDDEN:]
    o = _dot(act.astype(BF16), wout_ref[...])
    o_ref[...] = x + 0.5 * mod_ref[k0 + 2:k0 + 3, :] * o


def _ffn(x, mods, ln_g, w_in, w_out, layer, second, pop):
    tm = 512
    rows = x.shape[0]
    kern = functools.partial(_ffn_kernel, k0=6 if second else 0, gk=2 if second else 0)
    return pl.pallas_call(
        kern,
        grid=(rows // tm,),
        in_specs=[pl.BlockSpec((tm, D_MODEL), lambda t: (t, 0)),
                  _mod_spec(layer, pop, tm),
                  _ln_spec(layer),
                  _resident((None, D_MODEL, 2 * FFN_HIDDEN), lambda t: (layer, 0, 0)),
                  _resident((None, FFN_HIDDEN, D_MODEL), lambda t: (layer, 0, 0))],
        out_specs=pl.BlockSpec((tm, D_MODEL), lambda t: (t, 0)),
        out_shape=jax.ShapeDtypeStruct((rows, D_MODEL), F32),
        compiler_params=_cparams(1),
        name="ffn",
    )(x, mods, ln_g, w_in, w_out)


def _modlinear_kernel(x_ref, mod_ref, g_ref, w_ref, *o_refs, widths):
    h = _modnorm(x_ref[...], g_ref[1:2, :], mod_ref[3:4, :], mod_ref[4:5, :]).astype(BF16)
    off = 0
    for o_ref, wd in zip(o_refs, widths):
        o_ref[...] = _dot(h, w_ref[:, off:off + wd]).astype(o_ref.dtype)
        off += wd


def _modlinear(x, mods, ln_g, w, layer, j, pop, widths):
    tm = 512
    rows = x.shape[0]
    n = sum(widths)
    return pl.pallas_call(
        functools.partial(_modlinear_kernel, widths=widths),
        grid=(rows // tm,),
        in_specs=[pl.BlockSpec((tm, D_MODEL), lambda t: (t, 0)),
                  _mod_spec(layer, pop, tm),
                  _ln_spec(layer),
                  _resident((None, D_MODEL, n), lambda t: (j, 0, 0))],
        out_specs=[pl.BlockSpec((tm, wd), lambda t: (t, 0)) for wd in widths],
        out_shape=[jax.ShapeDtypeStruct((rows, wd), F32) for wd in widths],
        compiler_params=_cparams(1),
        name="modlinear",
    )(x, mods, ln_g, w)


def _linear_res_kernel(y_ref, x_ref, mod_ref, w_ref, o_ref):
    o_ref[...] = x_ref[...] + mod_ref[5:6, :] * _dot(y_ref[...], w_ref[...])


def _linear_res(y, x, mods, w, layer, j, pop):
    tm = 512
    rows, k = y.shape
    return pl.pallas_call(
        _linear_res_kernel,
        grid=(rows // tm,),
        in_specs=[pl.BlockSpec((tm, k), lambda t: (t, 0)),
                  pl.BlockSpec((tm, D_MODEL), lambda t: (t, 0)),
                  _mod_spec(layer, pop, tm),
                  _resident((None, k, D_MODEL), lambda t: (j, 0, 0))],
        out_specs=pl.BlockSpec((tm, D_MODEL), lambda t: (t, 0)),
        out_shape=jax.ShapeDtypeStruct((rows, D_MODEL), F32),
        compiler_params=_cparams(1),
        name="linear_res",
    )(y, x, mods, w)


def _tri_matmul(tri, v):
    hi, mid, lo = _split3(v)
    return _dot(tri, hi) + _dot(tri, mid) + _dot(tri, lo)


def _ssd_kernel(*refs, seq_len, has_h0, emit_state):
    it = iter(refs)
    x_ref, b_ref, c_ref, z_ref, dt_ref = (next(it) for _ in range(5))
    cwx_ref, cwb_ref, cwc_ref, cbx_ref, cbb_ref, cbc_ref = (next(it) for _ in range(6))
    dtb_ref, alog_ref, dsk_ref, ng_ref = (next(it) for _ in range(4))
    h0_ref = next(it) if has_h0 else None
    y_ref = next(it)
    st_ref = next(it) if emit_state else None
    xs_s, b_s, c_s, dt_s, dta_s, yacc_s, h_s = (next(it) for _ in range(7))
    n_chunks = seq_len // M_CHUNK

    def conv_silu(ref, w_ref, bias_ref):
        v = ref[...]
        t = lax.broadcasted_iota(jnp.int32, v.shape, 0)
        prev = jnp.where(t == 0, 0.0, pltpu.roll(v, 1, 0))
        nxt = jnp.where(t == seq_len - 1, 0.0, pltpu.roll(v, seq_len - 1, 0))
        w = w_ref[...]
        return _silu(prev * w[0:1, :] + v * w[1:2, :] + nxt * w[2:3, :] + bias_ref[...])

    xs = conv_silu(x_ref, cwx_ref, cbx_ref)
    xs_s[...] = xs
    b_s[...] = conv_silu(b_ref, cwb_ref, cbb_ref).astype(BF16)
    c_s[...] = conv_silu(c_ref, cwc_ref, cbc_ref).astype(BF16)
    dt = _softplus(dt_ref[...] + dtb_ref[...])
    dt_s[...] = dt
    dta_s[...] = dt * (-jnp.exp(alog_ref[...]))
    yacc_s[...] = xs * dsk_ref[...]
    if has_h0:
        h_s[...] = h0_ref[...]
    else:
        h_s[...] = jnp.zeros(h_s.shape, F32)

    row = lax.broadcasted_iota(jnp.int32, (M_CHUNK, M_CHUNK), 0)
    col = lax.broadcasted_iota(jnp.int32, (M_CHUNK, M_CHUNK), 1)
    first_half_lanes = col < M_HEADDIM
    first_half_rows = row < M_HEADDIM

    def chunk_step(c, d):
        rows = pl.ds(pl.multiple_of(c * M_CHUNK, M_CHUNK), M_CHUNK)
        keep = (col <= row) if d == 0 else (col >= row)
        end = M_CHUNK - 1 if d == 0 else 0
        dtc = dt_s[rows, :]
        acum = _tri_matmul(jnp.where(keep, 1.0, 0.0).astype(BF16), dta_s[rows, :])
        acum_t = acum.T
        dt_t = dtc.T
        bc = b_s[rows, :]
        cc = c_s[rows, :]
        cb = _dot_nt(cc, bc)
        for k in range(M_HEAD_PAIRS):
            xsl = xs_s[rows, k * LANES:(k + 1) * LANES]
            lanes = (8 * d + 2 * k, 8 * d + 2 * k + 1)
            a_col = [acum[:, ln:ln + 1] for ln in lanes]
            a_end = [acum[end:end + 1, ln:ln + 1] for ln in lanes]
            m = []
            for h, ln in enumerate(lanes):
                seg = a_col[h] - acum_t[ln:ln + 1, :]
                decay = jnp.exp(jnp.where(keep, seg, -jnp.inf))
                m.append((cb * decay * dt_t[ln:ln + 1, :]).astype(BF16))
            x0 = jnp.where(first_half_lanes, xsl, 0.0).astype(BF16)
            x1 = jnp.where(first_half_lanes, 0.0, xsl).astype(BF16)
            y_diag = _dot(m[0], x0) + _dot(m[1], x1)
            w = [jnp.exp(a_end[h] - a_col[h]) * dtc[:, ln:ln + 1] for h, ln in enumerate(lanes)]
            xw = xsl * jnp.where(first_half_lanes, w[0], w[1])
            s_new = _dot(xw.T.astype(BF16), bc)
            h_in = h_s[d, k]
            y_off = _dot_nt(cc, h_in.astype(BF16))
            e_col = jnp.where(first_half_lanes, jnp.exp(a_col[0]), jnp.exp(a_col[1]))
            yacc_s[rows, k * LANES:(k + 1) * LANES] += y_diag + e_col * y_off
            e_end = jnp.where(first_half_rows, jnp.exp(a_end[0]), jnp.exp(a_end[1]))
            h_s[d, k] = h_in * e_end + s_new

    def fwd_body(i, carry):
        chunk_step(i, 0)
        return carry

    def bwd_body(i, carry):
        chunk_step(n_chunks - 1 - i, 1)
        return carry

    lax.fori_loop(0, n_chunks, fwd_body, 0)
    lax.fori_loop(0, n_chunks, bwd_body, 0)

    y = yacc_s[...] * _silu(z_ref[...])
    y = y * lax.rsqrt(jnp.mean(y * y, axis=-1, keepdims=True) + EPS) * ng_ref[...]
    y_ref[...] = y.astype(BF16)
    if emit_state:
        st_ref[...] = h_s[...]


def _ssd(xbc, z, dt, conv_w, conv_b, dtb, alog, dsk, norm_g, h0, j, seq_len, emit_state):
    rows = xbc.shape[0]
    n_seq = rows // seq_len
    gx = M_INNER // LANES
    gc = gx + M_GROUPS * M_STATE // LANES
    has_h0 = h0 is not None
    st_block = (None, 2, M_HEAD_PAIRS, LANES, M_STATE)

    in_specs = [pl.BlockSpec((seq_len, M_GROUP_CH), lambda s, g: (s, g)),
                pl.BlockSpec((seq_len, M_STATE), lambda s, g: (s, gx + g)),
                pl.BlockSpec((seq_len, M_STATE), lambda s, g: (s, gc + g)),
                pl.BlockSpec((seq_len, M_GROUP_CH), lambda s, g: (s, g)),
                pl.BlockSpec((seq_len, LANES), lambda s, g: (s, g)),
                pl.BlockSpec((None, M_CONV, M_GROUP_CH), lambda s, g: (j, 0, g)),
                pl.BlockSpec((None, M_CONV, M_STATE), lambda s, g: (j, 0, gx + g)),
                pl.BlockSpec((None, M_CONV, M_STATE), lambda s, g: (j, 0, gc + g)),
                pl.BlockSpec((None, 1, M_GROUP_CH), lambda s, g: (j, 0, g)),
                pl.BlockSpec((None, 1, M_STATE), lambda s, g: (j, 0, gx + g)),
                pl.BlockSpec((None, 1, M_STATE), lambda s, g: (j, 0, gc + g)),
                pl.BlockSpec((None, 1, LANES), lambda s, g: (j, 0, g)),
                pl.BlockSpec((None, 1, LANES), lambda s, g: (j, 0, g)),
                pl.BlockSpec((None, 1, M_GROUP_CH), lambda s, g: (j, 0, g)),
                pl.BlockSpec((None, 1, M_GROUP_CH), lambda s, g: (j, 0, g))]
    args = [xbc, xbc, xbc, z, dt, conv_w, conv_w, conv_w, conv_b, conv_b, conv_b,
            dtb, alog, dsk, norm_g]
    if has_h0:
        in_specs.append(pl.BlockSpec(st_block, lambda s, g: (s, 0, g, 0, 0)))
        args.append(h0)
    out_specs = [pl.BlockSpec((seq_len, M_GROUP_CH), lambda s, g: (s, g))]
    out_shape = [jax.ShapeDtypeStruct((rows, M_INNER), BF16)]
    if emit_state:
        out_specs.append(pl.BlockSpec(st_block, lambda s, g: (s, 0, g, 0, 0)))
        out_shape.append(jax.ShapeDtypeStruct(
            (n_seq, 2, M_GROUPS * M_HEAD_PAIRS, LANES, M_STATE), F32))
    scratch = [pltpu.VMEM((seq_len, M_GROUP_CH), F32),
               pltpu.VMEM((seq_len, M_STATE), BF16),
               pltpu.VMEM((seq_len, M_STATE), BF16),
               pltpu.VMEM((seq_len, LANES), F32),
               pltpu.VMEM((seq_len, LANES), F32),
               pltpu.VMEM((seq_len, M_GROUP_CH), F32),
               pltpu.VMEM((2, M_HEAD_PAIRS, LANES, M_STATE), F32)]
    return pl.pallas_call(
        functools.partial(_ssd_kernel, seq_len=seq_len, has_h0=has_h0, emit_state=emit_state),
        grid=(n_seq, M_GROUPS),
        in_specs=in_specs,
        out_specs=out_specs,
        out_shape=out_shape,
        scratch_shapes=scratch,
        compiler_params=_cparams(2),
        name="ssd",
    )(*args)


def _gmlp_kernel(x_ref, mod_ref, g_ref, win_ref, bin_ref, ng_ref, ws_ref, bs_ref, wout_ref,
                 o_ref, gate_s, *, tm):
    x = x_ref[...]
    h = _modnorm(x, g_ref[1:2, :], mod_ref[3:4, :], mod_ref[4:5, :]).astype(BF16)
    hg = jax.nn.gelu(_dot(h, win_ref[...]) + bin_ref[...])
    u = hg[:, :G_INNER]
    v = hg[:, G_INNER:]
    v = v * lax.rsqrt(jnp.mean(v * v, axis=-1, keepdims=True) + EPS) * ng_ref[...]
    vb = v.astype(BF16)
    bs = bs_ref[...]
    for c in range(tm // G_CHUNK):
        r = slice(c * G_CHUNK, (c + 1) * G_CHUNK)
        for hd in range(G_HEADS):
            cs = slice(hd * G_HEAD_CH, (hd + 1) * G_HEAD_CH)
            sv = _dot(ws_ref[hd], vb[r, cs]) + bs[:, hd:hd + 1]
            gate_s[r, cs] = (u[r, cs] * sv).astype(BF16)
    o_ref[...] = x + mod_ref[5:6, :] * _dot(gate_s[...], wout_ref[...])


def _gmlp(x, mods, ln_g, w_in, b_in, norm_g, w_s, b_s_t, w_out, layer, j, pop):
    tm = 256
    rows = x.shape[0]
    return pl.pallas_call(
        functools.partial(_gmlp_kernel, tm=tm),
        grid=(rows // tm,),
        in_specs=[pl.BlockSpec((tm, D_MODEL), lambda t: (t, 0)),
                  _mod_spec(layer, pop, tm),
                  _ln_spec(layer),
                  _resident((None, D_MODEL, 2 * G_INNER), lambda t: (j, 0, 0)),
                  _resident((None, 1, 2 * G_INNER), lambda t: (j, 0, 0)),
                  _resident((None, 1, G_INNER), lambda t: (j, 0, 0)),
                  _resident((None, G_HEADS, G_CHUNK, G_CHUNK), lambda t: (j, 0, 0, 0)),
                  _resident((None, G_CHUNK, G_HEADS), lambda t: (j, 0, 0)),
                  _resident((None, G_INNER, D_MODEL), lambda t: (j, 0, 0))],
        out_specs=pl.BlockSpec((tm, D_MODEL), lambda t: (t, 0)),
        out_shape=jax.ShapeDtypeStruct((rows, D_MODEL), F32),
        scratch_shapes=[pltpu.VMEM((tm, G_INNER), BF16)],
        compiler_params=_cparams(1),
        name="gmlp",
    )(x, mods, ln_g, w_in, b_in, norm_g, w_s, b_s_t, w_out)


def _dft_tables(seq_len):
    def angles(n):
        k = np.arange(n, dtype=np.int64)
        return 2.0 * np.pi * ((k[:, None] * k[None, :]) % n).astype(np.float64) / n
    ac = angles(F_GROUP_CH)
    al = angles(seq_len)
    chan = np.concatenate([np.cos(ac), np.sin(ac)], axis=1).astype(np.float32)
    pos = np.concatenate([np.cos(al), -np.sin(al)], axis=1).astype(np.float32)
    return chan, pos


def _fnet_kernel(x_ref, mod_ref, g_ref, chan_ref, pos_ref, wout_ref, b_ref, o_ref, f_s,
                 *, seq_len):
    x = x_ref[...]
    h = _modnorm(x, g_ref[1:2, :], mod_ref[3:4, :], mod_ref[4:5, :]).astype(BF16)
    chan = chan_ref[...].astype(BF16)
    pos = pos_ref[...].astype(BF16)
    scale = 1.0 / math.sqrt(seq_len * F_GROUP_CH)
    for g in range(F_GROUPS):
        cs = slice(g * F_GROUP_CH, (g + 1) * F_GROUP_CH)
        p = _dot(h[:, cs], chan)
        stacked = jnp.concatenate([p[:, :F_GROUP_CH], p[:, F_GROUP_CH:]], axis=0).astype(BF16)
        f_s[:, cs] = (_dot(pos, stacked) * scale).astype(BF16)
    o_ref[...] = x + mod_ref[5:6, :] * (_dot(f_s[...], wout_ref[...]) + b_ref[...])


def _fnet(x, mods, ln_g, w_out, b_out, layer, j, pop, seq_len):
    rows = x.shape[0]
    chan, pos = _dft_tables(seq_len)
    return pl.pallas_call(
        functools.partial(_fnet_kernel, seq_len=seq_len),
        grid=(rows // seq_len,),
        in_specs=[pl.BlockSpec((seq_len, D_MODEL), lambda t: (t, 0)),
                  _mod_spec(layer, pop, seq_len),
                  _ln_spec(layer),
                  _resident(chan.shape, lambda t: (0, 0)),
                  _resident(pos.shape, lambda t: (0, 0)),
                  _resident((None, D_MODEL, D_MODEL), lambda t: (j, 0, 0)),
                  _resident((None, 1, D_MODEL), lambda t: (j, 0, 0))],
        out_specs=pl.BlockSpec((seq_len, D_MODEL), lambda t: (t, 0)),
        out_shape=jax.ShapeDtypeStruct((rows, D_MODEL), F32),
        scratch_shapes=[pltpu.VMEM((seq_len, D_MODEL), BF16)],
        compiler_params=_cparams(1),
        name="fnet",
    )(x, mods, ln_g, jnp.asarray(chan), jnp.asarray(pos), w_out, b_out)


def _head_mean_matrix():
    i = np.arange(2 * LANES)
    return ((i[:, None] // A_HD) == (i[None, :] // A_HD)).astype(np.float32) / A_HD


def _head_place_matrices():
    m = np.zeros((A_KV, A_KVW, A_GRP * A_HD), np.float32)
    d = np.arange(A_HD)
    for j in range(A_KV):
        for g in range(A_GRP):
            m[j, j * A_HD + d, g * A_HD + d] = 1.0
    return m


def _rope_tables():
    half = A_HD // 4
    inv = (np.float32(ROPE_THETA) ** (-np.arange(half, dtype=np.float32) / np.float32(half)))
    inv = inv.astype(np.float32)
    t = np.arange(DEC_SEQ)
    lane = np.arange(LANES)
    hl = lane % A_HD
    posn = np.where(hl[None, :] < A_HD // 2, (t // GRID_W)[:, None], (t % GRID_W)[:, None])
    sub = hl % (A_HD // 2)
    ang = posn.astype(np.float32) * inv[sub % half][None, :]
    cos = np.cos(ang).astype(np.float32)
    sin = np.sin(ang).astype(np.float32)
    lower = (sub < half)[None, :]
    sin_up = np.where(lower, -sin, 0.0).astype(np.float32)
    sin_dn = np.where(lower, 0.0, sin).astype(np.float32)
    return cos, sin_up, sin_dn


def _qkv_kernel(*refs, rope):
    it = iter(refs)
    x_ref, mod_ref, g_ref, w_ref, qg_ref, kg_ref, avg_ref = (next(it) for _ in range(7))
    if rope:
        cos_ref, sup_ref, sdn_ref = (next(it) for _ in range(3))
    q_ref, k_ref, v_ref = (next(it) for _ in range(3))
    h = _modnorm(x_ref[...], g_ref[1:2, :], mod_ref[3:4, :], mod_ref[4:5, :]).astype(BF16)
    qkv = _dot(h, w_ref[...])
    avg = avg_ref[...]
    half = A_HD // 4

    def head_norm(t, gain):
        hi, lo = _split2(t * t)
        ms = _dot(hi, avg) + _dot(lo, avg)
        return t * lax.rsqrt(ms + EPS) * gain

    def rotate(t):
        up = pltpu.roll(t, LANES - half, 1)
        dn = pltpu.roll(t, half, 1)
        return t * cos_ref[...] + up * sup_ref[...] + dn * sdn_ref[...]

    slab = 2 * LANES
    for s in range((A_Q + A_KVW) // slab):
        cs = slice(s * slab, (s + 1) * slab)
        is_q = s < A_Q // slab
        gain = qg_ref[:, cs] if is_q else kg_ref[...]
        t = head_norm(qkv[:, cs], gain)
        if rope:
            t = jnp.concatenate([rotate(t[:, :LANES]), rotate(t[:, LANES:])], axis=1)
        if is_q:
            q_ref[:, cs] = t.astype(q_ref.dtype)
        else:
            k_ref[...] = t.astype(k_ref.dtype)
    v_ref[...] = qkv[:, A_Q + A_KVW:].astype(v_ref.dtype)


def _qkv(x, mods, ln_g, w, q_gain, k_gain, layer, j, pop):
    tm = 512
    rows = x.shape[0]
    rope = pop == 1
    kv_dtype = BF16 if rope else F32
    in_specs = [pl.BlockSpec((tm, D_MODEL), lambda t: (t, 0)),
                _mod_spec(layer, pop, tm),
                _ln_spec(layer),
                _resident((None, D_MODEL, A_Q + 2 * A_KVW), lambda t: (j, 0, 0)),
                _resident((None, 1, A_Q), lambda t: (j, 0, 0)),
                _resident((None, 1, A_KVW), lambda t: (j, 0, 0)),
                _resident((2 * LANES, 2 * LANES), lambda t: (0, 0))]
    args = [x, mods, ln_g, w, q_gain, k_gain, jnp.asarray(_head_mean_matrix(), BF16)]
    if rope:
        per_seq = DEC_SEQ // tm
        in_specs += [pl.BlockSpec((tm, LANES), lambda t: (t % per_seq, 0))] * 3
        args += [jnp.asarray(a) for a in _rope_tables()]
    return pl.pallas_call(
        functools.partial(_qkv_kernel, rope=rope),
        grid=(rows // tm,),
        in_specs=in_specs,
        out_specs=[pl.BlockSpec((tm, A_Q), lambda t: (t, 0)),
                   pl.BlockSpec((tm, A_KVW), lambda t: (t, 0)),
                   pl.BlockSpec((tm, A_KVW), lambda t: (t, 0))],
        out_shape=[jax.ShapeDtypeStruct((rows, A_Q), BF16),
                   jax.ShapeDtypeStruct((rows, A_KVW), kv_dtype),
                   jax.ShapeDtypeStruct((rows, A_KVW), kv_dtype)],
        compiler_params=_cparams(1),
        name="qkv",
    )(*args)


def _attn_kernel(q_ref, place_ref, *refs, n_src):
    kv_refs, o_ref = refs[:2 * n_src], refs[2 * n_src]
    lane = lax.broadcasted_iota(jnp.int32, (1, A_GRP * A_HD), 1)
    scale = A_HD ** -0.5
    for j in range(A_KV):
        place = place_ref[j]
        qs = q_ref[:, j * A_GRP * A_HD:(j + 1) * A_GRP * A_HD]
        k4 = [_dot(kv_refs[2 * s][...].astype(BF16), place).astype(BF16) for s in range(n_src)]
        v4 = [_dot(kv_refs[2 * s + 1][...].astype(BF16), place).astype(BF16) for s in range(n_src)]
        acc = None
        for g in range(A_GRP):
            own = (lane >= g * A_HD) & (lane < (g + 1) * A_HD)
            qm = jnp.where(own, qs, jnp.zeros_like(qs))
            sc = [_dot_nt(qm, k4[s]) * scale for s in range(n_src)]
            mx = functools.reduce(jnp.maximum, [jnp.max(t, axis=-1, keepdims=True) for t in sc])
            p = [jnp.exp(t - mx) for t in sc]
            den = functools.reduce(jnp.add, [jnp.sum(t, axis=-1, keepdims=True) for t in p])
            pv = functools.reduce(jnp.add, [
                _dot(p[s].astype(BF16), jnp.where(own, v4[s], jnp.zeros_like(v4[s])))
                for s in range(n_src)])
            acc = pv / den if acc is None else acc + pv / den
        o_ref[:, j * A_GRP * A_HD:(j + 1) * A_GRP * A_HD] = acc.astype(o_ref.dtype)


def _attn(q, sources, seq_len, tq):
    rows = q.shape[0]
    n_batch = rows // seq_len
    per = seq_len // tq
    in_specs = [pl.BlockSpec((tq, A_Q), lambda b, t: (b * per + t, 0)),
                _resident((A_KV, A_KVW, A_GRP * A_HD), lambda b, t: (0, 0, 0))]
    args = [q, jnp.asarray(_head_place_matrices(), BF16)]
    for k, v, lk in sources:
        in_specs += [pl.BlockSpec((lk, A_KVW), lambda b, t: (b, 0))] * 2
        args += [k, v]
    return pl.pallas_call(
        functools.partial(_attn_kernel, n_src=len(sources)),
        grid=(n_batch, per),
        in_specs=in_specs,
        out_specs=pl.BlockSpec((tq, A_Q), lambda b, t: (b * per + t, 0)),
        out_shape=jax.ShapeDtypeStruct((rows, A_Q), BF16),
        compiler_params=_cparams(2),
        name="attn",
    )(*args)


def _regroup_heads(a):
    lead = a.shape[:-1]
    a = a.reshape(lead + (2, M_GROUPS, M_GROUP_HEADS))
    a = jnp.swapaxes(a, -3, -2).reshape(lead + (M_GROUPS, 2 * M_GROUP_HEADS))
    pad = [(0, 0)] * (len(lead) + 1) + [(0, LANES - 2 * M_GROUP_HEADS)]
    return jnp.pad(a, pad).reshape(lead + (M_GROUPS * LANES,))


def kernel(x_prompt, x_sample, state_ssm, cache_k, cache_v, c, c_ctx,
           ln_g, ada_w, ada_b, ff1_w_in, ff1_w_out, ff2_w_in, ff2_w_out,
           m_w_in, m_conv_w, m_conv_b, m_dt_bias, m_a_log, m_d, m_norm_g, m_w_out,
           g_w_in, g_b_in, g_norm_g, g_w_s, g_b_s, g_w_out,
           f_w_out, f_b_out,
           a_w_qkv, a_q_norm, a_k_norm, a_w_o):
    xs = [x_prompt.reshape(TP, D_MODEL), x_sample.reshape(TS, D_MODEL)]
    seq = (SEQ, DEC_SEQ)

    conds = jnp.concatenate(
        [c_ctx[None], c, jnp.zeros((COND_ROWS - 1 - DEC_BATCH, D_MODEL), F32)], axis=0)
    mods = _adaln(conds, ada_w, ada_b)

    ff_w = [(ff1_w_in.astype(BF16), ff1_w_out.astype(BF16)),
            (ff2_w_in.astype(BF16), ff2_w_out.astype(BF16))]

    new_ssm, new_k, new_v = [], [], []
    for i in range(DEPTH):
        kind, j = i % N_MIXERS, i // N_MIXERS
        xs = [_ffn(xs[p], mods, ln_g, ff_w[0][0], ff_w[0][1], i, False, p) for p in range(2)]
        if kind == 0:
            n_zx = M_INNER + M_CONV_CH
            w_in = jnp.concatenate(
                [m_w_in[..., :n_zx], _regroup_heads(m_w_in[..., n_zx:])], axis=-1).astype(BF16)
            w_out = m_w_out.astype(BF16)
            dtb = _regroup_heads(m_dt_bias.reshape(-1, 1, 2 * M_HEADS))
            alog = _regroup_heads(m_a_log.reshape(-1, 1, 2 * M_HEADS))
            dsk = jnp.repeat(m_d, M_HEADDIM, axis=-1)[:, None, :]
            conv_b = m_conv_b[:, None, :]
            norm_g = m_norm_g[:, None, :]
            h0 = state_ssm[:, j].reshape(
                DEC_BATCH, 2, M_GROUPS * M_HEAD_PAIRS, LANES, M_STATE)
            for p in range(2):
                z, xbc, dt = _modlinear(xs[p], mods, ln_g, w_in, i, j, p,
                                        (M_INNER, M_CONV_CH, M_GROUPS * LANES))
                res = _ssd(xbc, z, dt, m_conv_w, conv_b, dtb, alog, dsk, norm_g,
                           h0 if p == 1 else None, j, seq[p], emit_state=(p == 0))
                if p == 0:
                    new_ssm.append(res[1].reshape(BATCH, 2, M_HEADS, M_HEADDIM, M_STATE))
                xs[p] = _linear_res(res[0], xs[p], mods, w_out, i, j, p)
        elif kind == 1:
            w_in, w_out, w_s = g_w_in.astype(BF16), g_w_out.astype(BF16), g_w_s.astype(BF16)
            b_s_t = jnp.swapaxes(g_b_s, -1, -2)
            xs = [_gmlp(xs[p], mods, ln_g, w_in, g_b_in[:, None, :], g_norm_g[:, None, :],
                        w_s, b_s_t, w_out, i, j, p) for p in range(2)]
        elif kind == 2:
            w_out = f_w_out.astype(BF16)
            xs = [_fnet(xs[p], mods, ln_g, w_out, f_b_out[:, None, :], i, j, p, seq[p])
                  for p in range(2)]
        else:
            w_qkv, w_o = a_w_qkv.astype(BF16), a_w_o.astype(BF16)
            q_gain = jnp.tile(a_q_norm, (1, A_HEADS))[:, None, :]
            k_gain = jnp.tile(a_k_norm, (1, A_KV))[:, None, :]
            qp, kp, vp = _qkv(xs[0], mods, ln_g, w_qkv, q_gain, k_gain, i, j, 0)
            new_k.append(kp.reshape(BATCH, SEQ, A_KV, A_HD))
            new_v.append(vp.reshape(BATCH, SEQ, A_KV, A_HD))
            op = _attn(qp, [(kp, vp, SEQ)], SEQ, SEQ)
            ql, kl, vl = _qkv(xs[1], mods, ln_g, w_qkv, q_gain, k_gain, i, j, 1)
            kc = cache_k[:, j].reshape(DEC_BATCH * PAST_LEN, A_KVW)
            vc = cache_v[:, j].reshape(DEC_BATCH * PAST_LEN, A_KVW)
            ol = _attn(ql, [(kc, vc, PAST_LEN), (kl, vl, DEC_SEQ)], DEC_SEQ, 256)
            xs = [_linear_res(o, xs[p], mods, w_o, i, j, p) for p, o in enumerate((op, ol))]
        xs = [_ffn(xs[p], mods, ln_g, ff_w[1][0], ff_w[1][1], i, True, p) for p in range(2)]

    return (xs[0].reshape(BATCH, SEQ, D_MODEL),
            xs[1].reshape(DEC_BATCH, DEC_SEQ, D_MODEL),
            jnp.stack(new_ssm, axis=1),
            jnp.stack(new_k, axis=1),
            jnp.stack(new_v, axis=1))
```

```python
import functools
import math

import numpy as np
import jax
import jax.numpy as jnp
from jax import lax
from jax.experimental import pallas as pl
from jax.experimental.pallas import tpu as pltpu

F32 = jnp.float32
BF16 = jnp.bfloat16

D_MODEL = 1024
BATCH = 32
SEQ = 256
DEPTH = 4
DEC_BATCH = 2
DEC_SEQ = 1024
PAST_LEN = 256
GRID_W = 64
N_MIXERS = 4
N_MOD = 9
EPS = 1e-6
FFN_HIDDEN = 2816

M_INNER = 2 * D_MODEL
M_HEADDIM = 64
M_HEADS = M_INNER // M_HEADDIM
M_GROUPS = 4
M_STATE = 128
M_CONV = 3
M_CHUNK = 128
M_CONV_CH = M_INNER + 2 * M_GROUPS * M_STATE
M_GROUP_CH = M_INNER // M_GROUPS
M_GROUP_HEADS = M_HEADS // M_GROUPS

G_CHUNK = 128
G_INNER = 2 * D_MODEL
G_HEADS = 8
G_HEAD_CH = G_INNER // G_HEADS

F_GROUPS = 4
F_GROUP_CH = D_MODEL // F_GROUPS

A_HEADS = 16
A_KV = 4
A_HD = 64
A_GRP = A_HEADS // A_KV
A_Q = A_HEADS * A_HD
A_KVW = A_KV * A_HD
ROPE_THETA = 10000.0

LANES = 128
COND_ROWS = 8
TP = BATCH * SEQ
TS = DEC_BATCH * DEC_SEQ
VMEM_LIMIT = 56 * 2**20


def _cparams(n_grid):
    return pltpu.CompilerParams(dimension_semantics=("arbitrary",) * n_grid,
                                vmem_limit_bytes=VMEM_LIMIT)


def _resident(block_shape, index_map):
    return pl.BlockSpec(block_shape, index_map, pipeline_mode=pl.Buffered(1))


def _dot(a, b):
    return jnp.dot(a, b, preferred_element_type=F32)


def _dot_nt(a, b):
    return lax.dot_general(a, b, (((1,), (1,)), ((), ())), preferred_element_type=F32)


def _silu(x):
    return x * jax.nn.sigmoid(x)


def _softplus(x):
    return jnp.maximum(x, 0.0) + jnp.log1p(jnp.exp(-jnp.abs(x)))


def _modnorm(x, g, shift, scale):
    y = x * lax.rsqrt(jnp.mean(x * x, axis=-1, keepdims=True) + EPS) * g
    return y * (1.0 + scale) + shift


def _split3(v):
    hi = v.astype(BF16)
    r1 = v - hi.astype(F32)
    mid = r1.astype(BF16)
    lo = (r1 - mid.astype(F32)).astype(BF16)
    return hi, mid, lo


def _split2(v):
    hi = v.astype(BF16)
    return hi, (v - hi.astype(F32)).astype(BF16)


def _adaln_kernel(c_ref, w_ref, b_ref, o_ref):
    s = _silu(c_ref[...]).astype(BF16)
    o_ref[...] = _dot(s, w_ref[...].astype(BF16)) + b_ref[...]


def _adaln(conds, ada_w, ada_b):
    tn = 2304
    n = N_MOD * D_MODEL
    out = pl.pallas_call(
        _adaln_kernel,
        grid=(DEPTH, n // tn),
        in_specs=[pl.BlockSpec((COND_ROWS, D_MODEL), lambda i, j: (0, 0)),
                  pl.BlockSpec((None, D_MODEL, tn), lambda i, j: (i, 0, j)),
                  pl.BlockSpec((None, 1, tn), lambda i, j: (i, 0, j))],
        out_specs=pl.BlockSpec((None, COND_ROWS, tn), lambda i, j: (i, 0, j)),
        out_shape=jax.ShapeDtypeStruct((DEPTH, COND_ROWS, n), F32),
        compiler_params=_cparams(2),
        name="adaln",
    )(conds, ada_w, ada_b.reshape(DEPTH, 1, n))
    return out.reshape(DEPTH, COND_ROWS, N_MOD, D_MODEL)


def _mod_spec(layer, pop, tm):
    if pop == 0:
        return pl.BlockSpec((None, None, N_MOD, D_MODEL), lambda t, *_: (layer, 0, 0, 0))
    return pl.BlockSpec((None, None, N_MOD, D_MODEL),
                        lambda t, *_: (layer, 1 + (t * tm) // DEC_SEQ, 0, 0))


def _ln_spec(layer):
    return pl.BlockSpec((None, 3, D_MODEL), lambda t, *_: (layer, 0, 0))


def _ffn_half_step(x, mod_ref, g_ref, win_ref, wout_ref, o_ref, k0, gk):
    h = _modnorm(x, g_ref[gk:gk + 1, :], mod_ref[k0:k0 + 1, :], mod_ref[k0 + 1:k0 + 2, :])
    gu = _dot(h.astype(BF16), win_ref[...])
    act = _silu(gu[:, :FFN_HIDDEN]) * gu[:, FFN_HIDDEN:]
    o = _dot(act.astype(BF16), wout_ref[...])
    o_ref[...] = x + 0.5 * mod_ref[k0 + 2:k0 + 3, :] * o


def _ffn_kernel(x_ref, mod_ref, g_ref, win_ref, wout_ref, o_ref, *, k0, gk):
    _ffn_half_step(x_ref[...], mod_ref, g_ref, win_ref, wout_ref, o_ref, k0, gk)


def _mix_ffn_kernel(y_ref, wmix_ref, x_ref, mod_ref, g_ref, win_ref, wout_ref, o_ref, *, k0, gk):
    x = x_ref[...] + mod_ref[5:6, :] * _dot(y_ref[...], wmix_ref[...])
    _ffn_half_step(x, mod_ref, g_ref, win_ref, wout_ref, o_ref, k0, gk)


def _ffn(x, mods, ln_g, w_in, w_out, layer, second, pop, mix=None):
    tm = 512
    rows = x.shape[0]
    statics = dict(k0=6 if second else 0, gk=2 if second else 0)
    in_specs = [pl.BlockSpec((tm, D_MODEL), lambda t: (t, 0)),
                _mod_spec(layer, pop, tm),
                _ln_spec(layer),
                _resident((None, D_MODEL, 2 * FFN_HIDDEN), lambda t: (layer, 0, 0)),
                _resident((None, FFN_HIDDEN, D_MODEL), lambda t: (layer, 0, 0))]
    args = [x, mods, ln_g, w_in, w_out]
    kern = _ffn_kernel
    if mix is not None:
        y, w_mix, j = mix
        k = y.shape[1]
        in_specs = [pl.BlockSpec((tm, k), lambda t: (t, 0)),
                    _resident((None, k, D_MODEL), lambda t: (j, 0, 0))] + in_specs
        args = [y, w_mix] + args
        kern = _mix_ffn_kernel
    return pl.pallas_call(
        functools.partial(kern, **statics),
        grid=(rows // tm,),
        in_specs=in_specs,
        out_specs=pl.BlockSpec((tm, D_MODEL), lambda t: (t, 0)),
        out_shape=jax.ShapeDtypeStruct((rows, D_MODEL), F32),
        compiler_params=_cparams(1),
        name="ffn",
    )(*args)


def _modlinear_kernel(x_ref, mod_ref, g_ref, w_ref, *o_refs, widths):
    h = _modnorm(x_ref[...], g_ref[1:2, :], mod_ref[3:4, :], mod_ref[4:5, :]).astype(BF16)
    off = 0
    for o_ref, wd in zip(o_refs, widths):
        o_ref[...] = _dot(h, w_ref[:, off:off + wd]).astype(o_ref.dtype)
        off += wd


def _modlinear(x, mods, ln_g, w, layer, j, pop, widths):
    tm = 512
    rows = x.shape[0]
    n = sum(widths)
    return pl.pallas_call(
        functools.partial(_modlinear_kernel, widths=widths),
        grid=(rows // tm,),
        in_specs=[pl.BlockSpec((tm, D_MODEL), lambda t: (t, 0)),
                  _mod_spec(layer, pop, tm),
                  _ln_spec(layer),
                  _resident((None, D_MODEL, n), lambda t: (j, 0, 0))],
        out_specs=[pl.BlockSpec((tm, wd), lambda t: (t, 0)) for wd in widths],
        out_shape=[jax.ShapeDtypeStruct((rows, wd), F32) for wd in widths],
        compiler_params=_cparams(1),
        name="modlinear",
    )(x, mods, ln_g, w)


def _tri_matmul(tri, v):
    hi, mid, lo = _split3(v)
    return _dot(tri, hi) + _dot(tri, mid) + _dot(tri, lo)


def _ssd_kernel(*refs, seq_len, has_h0, emit_state):
    it = iter(refs)
    x_ref, b_ref, c_ref, z_ref, dt_ref = (next(it) for _ in range(5))
    cwx_ref, cwb_ref, cwc_ref, cbx_ref, cbb_ref, cbc_ref = (next(it) for _ in range(6))
    dtb_ref, alog_ref, dsk_ref, ng_ref = (next(it) for _ in range(4))
    h0_ref = next(it) if has_h0 else None
    y_ref = next(it)
    st_ref = next(it) if emit_state else None
    xs_s, xt_s, b_s, c_s, dta_s, dtt_s, dtat_s, yt_s, h_s = (next(it) for _ in range(9))
    n_chunks = seq_len // M_CHUNK

    def conv_silu(ref, w_ref, bias_ref):
        v = ref[...]
        t = lax.broadcasted_iota(jnp.int32, v.shape, 0)
        prev = jnp.where(t == 0, 0.0, pltpu.roll(v, 1, 0))
        nxt = jnp.where(t == seq_len - 1, 0.0, pltpu.roll(v, seq_len - 1, 0))
        w = w_ref[...]
        return _silu(prev * w[0:1, :] + v * w[1:2, :] + nxt * w[2:3, :] + bias_ref[...])

    xs = conv_silu(x_ref, cwx_ref, cbx_ref)
    xs_s[...] = xs
    b_s[...] = conv_silu(b_ref, cwb_ref, cbb_ref).astype(BF16)
    c_s[...] = conv_silu(c_ref, cwc_ref, cbc_ref).astype(BF16)
    dt = _softplus(dt_ref[...] + dtb_ref[...])
    dta = dt * (-jnp.exp(alog_ref[...]))
    dta_s[...] = dta
    for c in range(n_chunks):
        r = slice(c * M_CHUNK, (c + 1) * M_CHUNK)
        dtt_s[c] = dt[r, :].T
        dtat_s[c] = dta[r, :].T
        for k in range(M_GROUP_CH // LANES):
            xt_s[c, k * LANES:(k + 1) * LANES, :] = xs[r, k * LANES:(k + 1) * LANES].T
    yt_s[...] = jnp.zeros(yt_s.shape, F32)
    if has_h0:
        h_s[...] = h0_ref[...]
    else:
        h_s[...] = jnp.zeros(h_s.shape, F32)

    row = lax.broadcasted_iota(jnp.int32, (M_CHUNK, M_CHUNK), 0)
    col = lax.broadcasted_iota(jnp.int32, (M_CHUNK, M_CHUNK), 1)

    def head_rows(v8):
        return jnp.concatenate(
            [jnp.broadcast_to(v8[r:r + 1, :], (M_HEADDIM, v8.shape[1]))
             for r in range(M_GROUP_HEADS)], axis=0)

    def chunk_step(c, d):
        start = c * M_CHUNK
        rows = pl.ds(start if isinstance(c, int) else pl.multiple_of(start, M_CHUNK), M_CHUNK)
        keep = (col <= row) if d == 0 else (col >= row)
        end = M_CHUNK - 1 if d == 0 else 0
        tri = jnp.where(keep, 1.0, 0.0).astype(BF16)
        acum = _tri_matmul(tri, dta_s[rows, :])
        hi, mid, lo = _split3(dtat_s[c])
        acum_t = _dot_nt(hi, tri) + _dot_nt(mid, tri) + _dot_nt(lo, tri)
        hl = slice(M_GROUP_HEADS * d, M_GROUP_HEADS * (d + 1))
        a8 = acum_t[hl, :]
        dt8 = dtt_s[c, hl, :]
        a_end8 = jnp.broadcast_to(a8[:, end:end + 1], a8.shape)
        w8 = jnp.exp(a_end8 - a8) * dt8
        bc = b_s[rows, :]
        cc = c_s[rows, :]
        cb = _dot_nt(cc, bc)
        xt = xt_s[c]
        xtb = xt.astype(BF16)
        h_in = h_s[d]
        s_new = _dot((xt * head_rows(w8)).astype(BF16), bc)
        y_off = _dot_nt(h_in.astype(BF16), cc) * head_rows(jnp.exp(a8))
        y_diag = []
        for r in range(M_GROUP_HEADS):
            ln = M_GROUP_HEADS * d + r
            seg = jnp.broadcast_to(acum[:, ln:ln + 1], (M_CHUNK, M_CHUNK)) - a8[r:r + 1, :]
            decay = jnp.exp(jnp.where(keep, seg, -jnp.inf))
            m = (cb * decay * dt8[r:r + 1, :]).astype(BF16)
            y_diag.append(_dot_nt(xtb[r * M_HEADDIM:(r + 1) * M_HEADDIM, :], m))
        yt_s[c] += jnp.concatenate(y_diag, axis=0) + y_off
        h_s[d] = h_in * head_rows(jnp.exp(a_end8)) + s_new

    def both_directions(i):
        chunk_step(i, 0)
        chunk_step(n_chunks - 1 - i, 1)

    if n_chunks <= 2:
        for i in range(n_chunks):
            both_directions(i)
    else:
        def body(i, carry):
            both_directions(i)
            return carry
        lax.fori_loop(0, n_chunks, body, 0)

    for c in range(n_chunks):
        r = slice(c * M_CHUNK, (c + 1) * M_CHUNK)
        for k in range(M_GROUP_CH // LANES):
            cs = slice(k * LANES, (k + 1) * LANES)
            xs_s[r, cs] = xs_s[r, cs] * dsk_ref[:, cs] + yt_s[c, cs, :].T
    y = xs_s[...] * _silu(z_ref[...])
    y = y * lax.rsqrt(jnp.mean(y * y, axis=-1, keepdims=True) + EPS) * ng_ref[...]
    y_ref[...] = y.astype(BF16)
    if emit_state:
        st_ref[...] = h_s[...]


def _ssd(xbc, z, dt, conv_w, conv_b, dtb, alog, dsk, norm_g, h0, j, seq_len, emit_state):
    rows = xbc.shape[0]
    n_seq = rows // seq_len
    gx = M_INNER // LANES
    gc = gx + M_GROUPS * M_STATE // LANES
    has_h0 = h0 is not None
    n_chunks = seq_len // M_CHUNK
    st_block = (None, 2, M_GROUP_CH, M_STATE)

    in_specs = [pl.BlockSpec((seq_len, M_GROUP_CH), lambda s, g: (s, g)),
                pl.BlockSpec((seq_len, M_STATE), lambda s, g: (s, gx + g)),
                pl.BlockSpec((seq_len, M_STATE), lambda s, g: (s, gc + g)),
                pl.BlockSpec((seq_len, M_GROUP_CH), lambda s, g: (s, g)),
                pl.BlockSpec((seq_len, LANES), lambda s, g: (s, g)),
                pl.BlockSpec((None, M_CONV, M_GROUP_CH), lambda s, g: (j, 0, g)),
                pl.BlockSpec((None, M_CONV, M_STATE), lambda s, g: (j, 0, gx + g)),
                pl.BlockSpec((None, M_CONV, M_STATE), lambda s, g: (j, 0, gc + g)),
                pl.BlockSpec((None, 1, M_GROUP_CH), lambda s, g: (j, 0, g)),
                pl.BlockSpec((None, 1, M_STATE), lambda s, g: (j, 0, gx + g)),
                pl.BlockSpec((None, 1, M_STATE), lambda s, g: (j, 0, gc + g)),
                pl.BlockSpec((None, 1, LANES), lambda s, g: (j, 0, g)),
                pl.BlockSpec((None, 1, LANES), lambda s, g: (j, 0, g)),
                pl.BlockSpec((None, 1, M_GROUP_CH), lambda s, g: (j, 0, g)),
                pl.BlockSpec((None, 1, M_GROUP_CH), lambda s, g: (j, 0, g))]
    args = [xbc, xbc, xbc, z, dt, conv_w, conv_w, conv_w, conv_b, conv_b, conv_b,
            dtb, alog, dsk, norm_g]
    if has_h0:
        in_specs.append(pl.BlockSpec(st_block, lambda s, g: (s, 0, g, 0)))
        args.append(h0)
    out_specs = [pl.BlockSpec((seq_len, M_GROUP_CH), lambda s, g: (s, g))]
    out_shape = [jax.ShapeDtypeStruct((rows, M_INNER), BF16)]
    if emit_state:
        out_specs.append(pl.BlockSpec(st_block, lambda s, g: (s, 0, g, 0)))
        out_shape.append(jax.ShapeDtypeStruct((n_seq, 2, M_INNER, M_STATE), F32))
    scratch = [pltpu.VMEM((seq_len, M_GROUP_CH), F32),
               pltpu.VMEM((n_chunks, M_GROUP_CH, M_CHUNK), F32),
               pltpu.VMEM((seq_len, M_STATE), BF16),
               pltpu.VMEM((seq_len, M_STATE), BF16),
               pltpu.VMEM((seq_len, LANES), F32),
               pltpu.VMEM((n_chunks, LANES, M_CHUNK), F32),
               pltpu.VMEM((n_chunks, LANES, M_CHUNK), F32),
               pltpu.VMEM((n_chunks, M_GROUP_CH, M_CHUNK), F32),
               pltpu.VMEM((2, M_GROUP_CH, M_STATE), F32)]
    return pl.pallas_call(
        functools.partial(_ssd_kernel, seq_len=seq_len, has_h0=has_h0, emit_state=emit_state),
        grid=(n_seq, M_GROUPS),
        in_specs=in_specs,
        out_specs=out_specs,
        out_shape=out_shape,
        scratch_shapes=scratch,
        compiler_params=_cparams(2),
        name="ssd",
    )(*args)


def _gmlp_kernel(x_ref, mod_ref, g_ref, win_ref, bin_ref, ng_ref, ws_ref, bs_ref, wout_ref,
                 o_ref, gate_s, *, tm):
    x = x_ref[...]
    h = _modnorm(x, g_ref[1:2, :], mod_ref[3:4, :], mod_ref[4:5, :]).astype(BF16)
    hg = jax.nn.gelu(_dot(h, win_ref[...]) + bin_ref[...])
    u = hg[:, :G_INNER]
    v = hg[:, G_INNER:]
    v = v * lax.rsqrt(jnp.mean(v * v, axis=-1, keepdims=True) + EPS) * ng_ref[...]
    vb = v.astype(BF16)
    bs = bs_ref[...]
    for c in range(tm // G_CHUNK):
        r = slice(c * G_CHUNK, (c + 1) * G_CHUNK)
        for hd in range(G_HEADS):
            cs = slice(hd * G_HEAD_CH, (hd + 1) * G_HEAD_CH)
            sv = _dot(ws_ref[hd], vb[r, cs]) + bs[:, hd:hd + 1]
            gate_s[r, cs] = (u[r, cs] * sv).astype(BF16)
    o_ref[...] = x + mod_ref[5:6, :] * _dot(gate_s[...], wout_ref[...])


def _gmlp(x, mods, ln_g, w_in, b_in, norm_g, w_s, b_s_t, w_out, layer, j, pop):
    tm = 512
    rows = x.shape[0]
    return pl.pallas_call(
        functools.partial(_gmlp_kernel, tm=tm),
        grid=(rows // tm,),
        in_specs=[pl.BlockSpec((tm, D_MODEL), lambda t: (t, 0)),
                  _mod_spec(layer, pop, tm),
                  _ln_spec(layer),
                  _resident((None, D_MODEL, 2 * G_INNER), lambda t: (j, 0, 0)),
                  _resident((None, 1, 2 * G_INNER), lambda t: (j, 0, 0)),
                  _resident((None, 1, G_INNER), lambda t: (j, 0, 0)),
                  _resident((None, G_HEADS, G_CHUNK, G_CHUNK), lambda t: (j, 0, 0, 0)),
                  _resident((None, G_CHUNK, G_HEADS), lambda t: (j, 0, 0)),
                  _resident((None, G_INNER, D_MODEL), lambda t: (j, 0, 0))],
        out_specs=pl.BlockSpec((tm, D_MODEL), lambda t: (t, 0)),
        out_shape=jax.ShapeDtypeStruct((rows, D_MODEL), F32),
        scratch_shapes=[pltpu.VMEM((tm, G_INNER), BF16)],
        compiler_params=_cparams(1),
        name="gmlp",
    )(x, mods, ln_g, w_in, b_in, norm_g, w_s, b_s_t, w_out)


def _dft_tables(seq_len):
    def angles(n):
        k = np.arange(n, dtype=np.int64)
        return 2.0 * np.pi * ((k[:, None] * k[None, :]) % n).astype(np.float64) / n
    ac = angles(F_GROUP_CH)
    al = angles(seq_len)
    chan = np.concatenate([np.cos(ac), np.sin(ac)], axis=1).astype(np.float32)
    pos = np.concatenate([np.cos(al), -np.sin(al)], axis=1).astype(np.float32)
    return chan, pos


def _fnet_kernel(x_ref, mod_ref, g_ref, chan_ref, pos_ref, wout_ref, b_ref, o_ref, f_s,
                 *, seq_len):
    x = x_ref[...]
    h = _modnorm(x, g_ref[1:2, :], mod_ref[3:4, :], mod_ref[4:5, :]).astype(BF16)
    chan = chan_ref[...].astype(BF16)
    pos = pos_ref[...].astype(BF16)
    scale = 1.0 / math.sqrt(seq_len * F_GROUP_CH)
    for g in range(F_GROUPS):
        cs = slice(g * F_GROUP_CH, (g + 1) * F_GROUP_CH)
        p = _dot(h[:, cs], chan)
        stacked = jnp.concatenate([p[:, :F_GROUP_CH], p[:, F_GROUP_CH:]], axis=0).astype(BF16)
        f_s[:, cs] = (_dot(pos, stacked) * scale).astype(BF16)
    o_ref[...] = x + mod_ref[5:6, :] * (_dot(f_s[...], wout_ref[...]) + b_ref[...])


def _fnet(x, mods, ln_g, w_out, b_out, layer, j, pop, seq_len):
    rows = x.shape[0]
    chan, pos = _dft_tables(seq_len)
    return pl.pallas_call(
        functools.partial(_fnet_kernel, seq_len=seq_len),
        grid=(rows // seq_len,),
        in_specs=[pl.BlockSpec((seq_len, D_MODEL), lambda t: (t, 0)),
                  _mod_spec(layer, pop, seq_len),
                  _ln_spec(layer),
                  _resident(chan.shape, lambda t: (0, 0)),
                  _resident(pos.shape, lambda t: (0, 0)),
                  _resident((None, D_MODEL, D_MODEL), lambda t: (j, 0, 0)),
                  _resident((None, 1, D_MODEL), lambda t: (j, 0, 0))],
        out_specs=pl.BlockSpec((seq_len, D_MODEL), lambda t: (t, 0)),
        out_shape=jax.ShapeDtypeStruct((rows, D_MODEL), F32),
        scratch_shapes=[pltpu.VMEM((seq_len, D_MODEL), BF16)],
        compiler_params=_cparams(1),
        name="fnet",
    )(x, mods, ln_g, jnp.asarray(chan), jnp.asarray(pos), w_out, b_out)


def _head_mean_matrix():
    i = np.arange(2 * LANES)
    return ((i[:, None] // A_HD) == (i[None, :] // A_HD)).astype(np.float32) / A_HD


def _head_place_matrices():
    m = np.zeros((A_KV, A_KVW, A_GRP * A_HD), np.float32)
    d = np.arange(A_HD)
    for j in range(A_KV):
        for g in range(A_GRP):
            m[j, j * A_HD + d, g * A_HD + d] = 1.0
    return m


def _rope_tables():
    half = A_HD // 4
    inv = (np.float32(ROPE_THETA) ** (-np.arange(half, dtype=np.float32) / np.float32(half)))
    inv = inv.astype(np.float32)
    t = np.arange(DEC_SEQ)
    lane = np.arange(LANES)
    hl = lane % A_HD
    posn = np.where(hl[None, :] < A_HD // 2, (t // GRID_W)[:, None], (t % GRID_W)[:, None])
    sub = hl % (A_HD // 2)
    ang = posn.astype(np.float32) * inv[sub % half][None, :]
    cos = np.cos(ang).astype(np.float32)
    sin = np.sin(ang).astype(np.float32)
    lower = (sub < half)[None, :]
    sin_up = np.where(lower, -sin, 0.0).astype(np.float32)
    sin_dn = np.where(lower, 0.0, sin).astype(np.float32)
    return cos, sin_up, sin_dn


def _qkv_kernel(*refs, rope):
    it = iter(refs)
    x_ref, mod_ref, g_ref, w_ref, qg_ref, kg_ref, avg_ref = (next(it) for _ in range(7))
    if rope:
        cos_ref, sup_ref, sdn_ref = (next(it) for _ in range(3))
    q_ref, k_ref, v_ref = (next(it) for _ in range(3))
    h = _modnorm(x_ref[...], g_ref[1:2, :], mod_ref[3:4, :], mod_ref[4:5, :]).astype(BF16)
    qkv = _dot(h, w_ref[...])
    avg = avg_ref[...]
    half = A_HD // 4

    def head_norm(t, gain):
        hi, lo = _split2(t * t)
        ms = _dot(hi, avg) + _dot(lo, avg)
        return t * lax.rsqrt(ms + EPS) * gain

    def rotate(t):
        up = pltpu.roll(t, LANES - half, 1)
        dn = pltpu.roll(t, half, 1)
        return t * cos_ref[...] + up * sup_ref[...] + dn * sdn_ref[...]

    slab = 2 * LANES
    for s in range((A_Q + A_KVW) // slab):
        cs = slice(s * slab, (s + 1) * slab)
        is_q = s < A_Q // slab
        gain = qg_ref[:, cs] if is_q else kg_ref[...]
        t = head_norm(qkv[:, cs], gain)
        if rope:
            t = jnp.concatenate([rotate(t[:, :LANES]), rotate(t[:, LANES:])], axis=1)
        if is_q:
            q_ref[:, cs] = t.astype(q_ref.dtype)
        else:
            k_ref[...] = t.astype(k_ref.dtype)
    v_ref[...] = qkv[:, A_Q + A_KVW:].astype(v_ref.dtype)


def _qkv(x, mods, ln_g, w, q_gain, k_gain, layer, j, pop):
    tm = 512
    rows = x.shape[0]
    rope = pop == 1
    kv_dtype = BF16 if rope else F32
    in_specs = [pl.BlockSpec((tm, D_MODEL), lambda t: (t, 0)),
                _mod_spec(layer, pop, tm),
                _ln_spec(layer),
                _resident((None, D_MODEL, A_Q + 2 * A_KVW), lambda t: (j, 0, 0)),
                _resident((None, 1, A_Q), lambda t: (j, 0, 0)),
                _resident((None, 1, A_KVW), lambda t: (j, 0, 0)),
                _resident((2 * LANES, 2 * LANES), lambda t: (0, 0))]
    args = [x, mods, ln_g, w, q_gain, k_gain, jnp.asarray(_head_mean_matrix(), BF16)]
    if rope:
        per_seq = DEC_SEQ // tm
        in_specs += [pl.BlockSpec((tm, LANES), lambda t: (t % per_seq, 0))] * 3
        args += [jnp.asarray(a) for a in _rope_tables()]
    return pl.pallas_call(
        functools.partial(_qkv_kernel, rope=rope),
        grid=(rows // tm,),
        in_specs=in_specs,
        out_specs=[pl.BlockSpec((tm, A_Q), lambda t: (t, 0)),
                   pl.BlockSpec((tm, A_KVW), lambda t: (t, 0)),
                   pl.BlockSpec((tm, A_KVW), lambda t: (t, 0))],
        out_shape=[jax.ShapeDtypeStruct((rows, A_Q), BF16),
                   jax.ShapeDtypeStruct((rows, A_KVW), kv_dtype),
                   jax.ShapeDtypeStruct((rows, A_KVW), kv_dtype)],
        compiler_params=_cparams(1),
        name="qkv",
    )(*args)


def _attn_kernel(q_ref, place_ref, *refs, n_src):
    kv_refs, o_ref = refs[:2 * n_src], refs[2 * n_src]
    lane = lax.broadcasted_iota(jnp.int32, (1, A_GRP * A_HD), 1)
    scale = A_HD ** -0.5
    for j in range(A_KV):
        place = place_ref[j]
        qs = q_ref[:, j * A_GRP * A_HD:(j + 1) * A_GRP * A_HD]
        k4 = [_dot(kv_refs[2 * s][...].astype(BF16), place).astype(BF16) for s in range(n_src)]
        v4 = [_dot(kv_refs[2 * s + 1][...].astype(BF16), place).astype(BF16) for s in range(n_src)]
        acc = None
        for g in range(A_GRP):
            own = (lane >= g * A_HD) & (lane < (g + 1) * A_HD)
            qm = jnp.where(own, qs, jnp.zeros_like(qs))
            sc = [_dot_nt(qm, k4[s]) * scale for s in range(n_src)]
            mx = functools.reduce(jnp.maximum, [jnp.max(t, axis=-1, keepdims=True) for t in sc])
            p = [jnp.exp(t - mx) for t in sc]
            den = functools.reduce(jnp.add, [jnp.sum(t, axis=-1, keepdims=True) for t in p])
            pv = functools.reduce(jnp.add, [
                _dot(p[s].astype(BF16), jnp.where(own, v4[s], jnp.zeros_like(v4[s])))
                for s in range(n_src)])
            acc = pv / den if acc is None else acc + pv / den
        o_ref[:, j * A_GRP * A_HD:(j + 1) * A_GRP * A_HD] = acc.astype(o_ref.dtype)


def _attn(q, sources, seq_len, tq):
    rows = q.shape[0]
    n_batch = rows // seq_len
    per = seq_len // tq
    in_specs = [pl.BlockSpec((tq, A_Q), lambda b, t: (b * per + t, 0)),
                _resident((A_KV, A_KVW, A_GRP * A_HD), lambda b, t: (0, 0, 0))]
    args = [q, jnp.asarray(_head_place_matrices(), BF16)]
    for k, v, lk in sources:
        in_specs += [pl.BlockSpec((lk, A_KVW), lambda b, t: (b, 0))] * 2
        args += [k, v]
    return pl.pallas_call(
        functools.partial(_attn_kernel, n_src=len(sources)),
        grid=(n_batch, per),
        in_specs=in_specs,
        out_specs=pl.BlockSpec((tq, A_Q), lambda b, t: (b * per + t, 0)),
        out_shape=jax.ShapeDtypeStruct((rows, A_Q), BF16),
        compiler_params=_cparams(2),
        name="attn",
    )(*args)


def _regroup_heads(a):
    lead = a.shape[:-1]
    a = a.reshape(lead + (2, M_GROUPS, M_GROUP_HEADS))
    a = jnp.swapaxes(a, -3, -2).reshape(lead + (M_GROUPS, 2 * M_GROUP_HEADS))
    pad = [(0, 0)] * (len(lead) + 1) + [(0, LANES - 2 * M_GROUP_HEADS)]
    return jnp.pad(a, pad).reshape(lead + (M_GROUPS * LANES,))


def kernel(x_prompt, x_sample, state_ssm, cache_k, cache_v, c, c_ctx,
           ln_g, ada_w, ada_b, ff1_w_in, ff1_w_out, ff2_w_in, ff2_w_out,
           m_w_in, m_conv_w, m_conv_b, m_dt_bias, m_a_log, m_d, m_norm_g, m_w_out,
           g_w_in, g_b_in, g_norm_g, g_w_s, g_b_s, g_w_out,
           f_w_out, f_b_out,
           a_w_qkv, a_q_norm, a_k_norm, a_w_o):
    xs = [x_prompt.reshape(TP, D_MODEL), x_sample.reshape(TS, D_MODEL)]
    seq = (SEQ, DEC_SEQ)

    conds = jnp.concatenate(
        [c_ctx[None], c, jnp.zeros((COND_ROWS - 1 - DEC_BATCH, D_MODEL), F32)], axis=0)
    mods = _adaln(conds, ada_w, ada_b)

    ff_w = [(ff1_w_in.astype(BF16), ff1_w_out.astype(BF16)),
            (ff2_w_in.astype(BF16), ff2_w_out.astype(BF16))]

    new_ssm, new_k, new_v = [], [], []
    for i in range(DEPTH):
        kind, j = i % N_MIXERS, i // N_MIXERS
        xs = [_ffn(xs[p], mods, ln_g, ff_w[0][0], ff_w[0][1], i, False, p) for p in range(2)]
        mix = [None, None]
        if kind == 0:
            n_zx = M_INNER + M_CONV_CH
            w_in = jnp.concatenate(
                [m_w_in[..., :n_zx], _regroup_heads(m_w_in[..., n_zx:])], axis=-1).astype(BF16)
            w_out = m_w_out.astype(BF16)
            dtb = _regroup_heads(m_dt_bias.reshape(-1, 1, 2 * M_HEADS))
            alog = _regroup_heads(m_a_log.reshape(-1, 1, 2 * M_HEADS))
            dsk = jnp.repeat(m_d, M_HEADDIM, axis=-1)[:, None, :]
            conv_b = m_conv_b[:, None, :]
            norm_g = m_norm_g[:, None, :]
            h0 = state_ssm[:, j].reshape(DEC_BATCH, 2, M_INNER, M_STATE)
            for p in range(2):
                z, xbc, dt = _modlinear(xs[p], mods, ln_g, w_in, i, j, p,
                                        (M_INNER, M_CONV_CH, M_GROUPS * LANES))
                res = _ssd(xbc, z, dt, m_conv_w, conv_b, dtb, alog, dsk, norm_g,
                           h0 if p == 1 else None, j, seq[p], emit_state=(p == 0))
                if p == 0:
                    new_ssm.append(res[1].reshape(BATCH, 2, M_HEADS, M_HEADDIM, M_STATE))
                mix[p] = (res[0], w_out, j)
        elif kind == 1:
            w_in, w_out, w_s = g_w_in.astype(BF16), g_w_out.astype(BF16), g_w_s.astype(BF16)
            b_s_t = jnp.swapaxes(g_b_s, -1, -2)
            xs = [_gmlp(xs[p], mods, ln_g, w_in, g_b_in[:, None, :], g_norm_g[:, None, :],
                        w_s, b_s_t, w_out, i, j, p) for p in range(2)]
        elif kind == 2:
            w_out = f_w_out.astype(BF16)
            xs = [_fnet(xs[p], mods, ln_g, w_out, f_b_out[:, None, :], i, j, p, seq[p])
                  for p in range(2)]
        else:
            w_qkv, w_o = a_w_qkv.astype(BF16), a_w_o.astype(BF16)
            q_gain = jnp.tile(a_q_norm, (1, A_HEADS))[:, None, :]
            k_gain = jnp.tile(a_k_norm, (1, A_KV))[:, None, :]
            qp, kp, vp = _qkv(xs[0], mods, ln_g, w_qkv, q_gain, k_gain, i, j, 0)
            new_k.append(kp.reshape(BATCH, SEQ, A_KV, A_HD))
            new_v.append(vp.reshape(BATCH, SEQ, A_KV, A_HD))
            op = _attn(qp, [(kp, vp, SEQ)], SEQ, SEQ)
            ql, kl, vl = _qkv(xs[1], mods, ln_g, w_qkv, q_gain, k_gain, i, j, 1)
            kc = cache_k[:, j].reshape(DEC_BATCH * PAST_LEN, A_KVW)
            vc = cache_v[:, j].reshape(DEC_BATCH * PAST_LEN, A_KVW)
            ol = _attn(ql, [(kc, vc, PAST_LEN), (kl, vl, DEC_SEQ)], DEC_SEQ, 256)
            mix = [(op, w_o, j), (ol, w_o, j)]
        xs = [_ffn(xs[p], mods, ln_g, ff_w[1][0], ff_w[1][1], i, True, p, mix[p])
              for p in range(2)]

    return (xs[0].reshape(BATCH, SEQ, D_MODEL),
            xs[1].reshape(DEC_BATCH, DEC_SEQ, D_MODEL),
            jnp.stack(new_ssm, axis=1),
            jnp.stack(new_k, axis=1),
            jnp.stack(new_v, axis=1))
```

```python
import functools
import math

import numpy as np
import jax
import jax.numpy as jnp
from jax import lax
from jax.experimental import pallas as pl
from jax.experimental.pallas import tpu as pltpu

F32 = jnp.float32
BF16 = jnp.bfloat16

D_MODEL = 1024
BATCH = 32
SEQ = 256
DEPTH = 4
DEC_BATCH = 2
DEC_SEQ = 1024
PAST_LEN = 256
GRID_W = 64
N_MIXERS = 4
N_MOD = 9
EPS = 1e-6
FFN_HIDDEN = 2816

M_INNER = 2 * D_MODEL
M_HEADDIM = 64
M_HEADS = M_INNER // M_HEADDIM
M_GROUPS = 4
M_STATE = 128
M_CONV = 3
M_CHUNK = 128
M_CONV_CH = M_INNER + 2 * M_GROUPS * M_STATE
M_GROUP_CH = M_INNER // M_GROUPS
M_GROUP_HEADS = M_HEADS // M_GROUPS

G_CHUNK = 128
G_INNER = 2 * D_MODEL
G_HEADS = 8
G_HEAD_CH = G_INNER // G_HEADS

F_GROUPS = 4
F_GROUP_CH = D_MODEL // F_GROUPS

A_HEADS = 16
A_KV = 4
A_HD = 64
A_GRP = A_HEADS // A_KV
A_Q = A_HEADS * A_HD
A_KVW = A_KV * A_HD
ROPE_THETA = 10000.0

LANES = 128
COND_ROWS = 8
TP = BATCH * SEQ
TS = DEC_BATCH * DEC_SEQ
VMEM_LIMIT = 56 * 2**20


def _cparams(n_grid):
    return pltpu.CompilerParams(dimension_semantics=("arbitrary",) * n_grid,
                                vmem_limit_bytes=VMEM_LIMIT)


def _resident(block_shape, index_map):
    return pl.BlockSpec(block_shape, index_map, pipeline_mode=pl.Buffered(1))


def _dot(a, b):
    return jnp.dot(a, b, preferred_element_type=F32)


def _dot_nt(a, b):
    return lax.dot_general(a, b, (((1,), (1,)), ((), ())), preferred_element_type=F32)


def _silu(x):
    return x * jax.nn.sigmoid(x)


def _softplus(x):
    return jnp.maximum(x, 0.0) + jnp.log1p(jnp.exp(-jnp.abs(x)))


def _modnorm(x, g, shift, scale):
    y = x * lax.rsqrt(jnp.mean(x * x, axis=-1, keepdims=True) + EPS) * g
    return y * (1.0 + scale) + shift


def _split3(v):
    hi = v.astype(BF16)
    r1 = v - hi.astype(F32)
    mid = r1.astype(BF16)
    lo = (r1 - mid.astype(F32)).astype(BF16)
    return hi, mid, lo


def _split2(v):
    hi = v.astype(BF16)
    return hi, (v - hi.astype(F32)).astype(BF16)


def _adaln_kernel(c_ref, w_ref, b_ref, o_ref):
    s = _silu(c_ref[...]).astype(BF16)
    o_ref[...] = _dot(s, w_ref[...].astype(BF16)) + b_ref[...]


def _adaln(conds, ada_w, ada_b):
    tn = 2304
    n = N_MOD * D_MODEL
    out = pl.pallas_call(
        _adaln_kernel,
        grid=(DEPTH, n // tn),
        in_specs=[pl.BlockSpec((COND_ROWS, D_MODEL), lambda i, j: (0, 0)),
                  pl.BlockSpec((None, D_MODEL, tn), lambda i, j: (i, 0, j)),
                  pl.BlockSpec((None, 1, tn), lambda i, j: (i, 0, j))],
        out_specs=pl.BlockSpec((None, COND_ROWS, tn), lambda i, j: (i, 0, j)),
        out_shape=jax.ShapeDtypeStruct((DEPTH, COND_ROWS, n), F32),
        compiler_params=_cparams(2),
        name="adaln",
    )(conds, ada_w, ada_b.reshape(DEPTH, 1, n))
    return out.reshape(DEPTH, COND_ROWS, N_MOD, D_MODEL)


def _mod_spec(layer, pop, tm):
    if pop == 0:
        return pl.BlockSpec((None, None, N_MOD, D_MODEL), lambda t, *_: (layer, 0, 0, 0))
    return pl.BlockSpec((None, None, N_MOD, D_MODEL),
                        lambda t, *_: (layer, 1 + (t * tm) // DEC_SEQ, 0, 0))


def _ln_spec(layer):
    return pl.BlockSpec((None, 3, D_MODEL), lambda t, *_: (layer, 0, 0))


MXU_COLS = 256
FFN_PIECES = ((0, 6 * MXU_COLS), (6 * MXU_COLS, FFN_HIDDEN))


def _ffn_kernel(*refs, k0, gk, has_mix, n_convert):
    it = iter(refs)
    if has_mix:
        y_ref, wmix_ref = next(it), next(it)
    x_ref, mod_ref, g_ref, win_ref, wout_ref = (next(it) for _ in range(5))
    src_refs = [next(it) for _ in range(2 * n_convert)]
    o_ref = next(it)
    dst_refs = [next(it) for _ in range(2 * n_convert)]

    x = x_ref[...]
    if has_mix:
        x = x + mod_ref[5:6, :] * _dot(y_ref[...], wmix_ref[...])
    h = _modnorm(x, g_ref[gk:gk + 1, :], mod_ref[k0:k0 + 1, :], mod_ref[k0 + 1:k0 + 2, :])
    h = h.astype(BF16)
    o = None
    for lo, hi in FFN_PIECES:
        gate = _dot(h, win_ref[:, lo:hi])
        up = _dot(h, win_ref[:, FFN_HIDDEN + lo:FFN_HIDDEN + hi])
        part = _dot((_silu(gate) * up).astype(BF16), wout_ref[lo:hi, :])
        o = part if o is None else o + part
    o_ref[...] = x + 0.5 * mod_ref[k0 + 2:k0 + 3, :] * o
    for src, dst in zip(src_refs, dst_refs):
        dst[...] = src[...].astype(BF16)


def _ffn(x, mods, ln_g, w_in, w_out, layer, second, pop, mix=None, convert=()):
    tm = 512
    rows = x.shape[0]
    steps = rows // tm
    in_specs = [pl.BlockSpec((tm, D_MODEL), lambda t: (t, 0)),
                _mod_spec(layer, pop, tm),
                _ln_spec(layer),
                _resident((D_MODEL, 2 * FFN_HIDDEN), lambda t: (0, 0)),
                _resident((FFN_HIDDEN, D_MODEL), lambda t: (0, 0))]
    args = [x, mods, ln_g, w_in, w_out]
    if mix is not None:
        y, w_mix, j = mix
        k = y.shape[1]
        in_specs = [pl.BlockSpec((tm, k), lambda t: (t, 0)),
                    _resident((None, k, D_MODEL), lambda t: (j, 0, 0))] + in_specs
        args = [y, w_mix] + args
    out_specs = [pl.BlockSpec((tm, D_MODEL), lambda t: (t, 0))]
    out_shape = [jax.ShapeDtypeStruct((rows, D_MODEL), F32)]
    for src_in, src_out, src_layer in convert:
        for src, n_rows, n_cols in ((src_in, D_MODEL, 2 * FFN_HIDDEN),
                                    (src_out, FFN_HIDDEN, D_MODEL)):
            slab = n_rows // steps
            in_specs.append(pl.BlockSpec((None, slab, n_cols),
                                         lambda t, src_layer=src_layer: (src_layer, t, 0)))
            args.append(src)
            out_specs.append(pl.BlockSpec((slab, n_cols), lambda t: (t, 0)))
            out_shape.append(jax.ShapeDtypeStruct((n_rows, n_cols), BF16))
    res = pl.pallas_call(
        functools.partial(_ffn_kernel, k0=6 if second else 0, gk=2 if second else 0,
                          has_mix=mix is not None, n_convert=len(convert)),
        grid=(steps,),
        in_specs=in_specs,
        out_specs=out_specs,
        out_shape=out_shape,
        compiler_params=_cparams(1),
        name="ffn",
    )(*args)
    return res[0], [(res[1 + 2 * n], res[2 + 2 * n]) for n in range(len(convert))]


def _modlinear_kernel(x_ref, mod_ref, g_ref, w_ref, *o_refs, widths):
    h = _modnorm(x_ref[...], g_ref[1:2, :], mod_ref[3:4, :], mod_ref[4:5, :]).astype(BF16)
    off = 0
    for o_ref, wd in zip(o_refs, widths):
        o_ref[...] = _dot(h, w_ref[:, off:off + wd]).astype(o_ref.dtype)
        off += wd


def _modlinear(x, mods, ln_g, w, layer, j, pop, widths):
    tm = 512
    rows = x.shape[0]
    n = sum(widths)
    return pl.pallas_call(
        functools.partial(_modlinear_kernel, widths=widths),
        grid=(rows // tm,),
        in_specs=[pl.BlockSpec((tm, D_MODEL), lambda t: (t, 0)),
                  _mod_spec(layer, pop, tm),
                  _ln_spec(layer),
                  _resident((None, D_MODEL, n), lambda t: (j, 0, 0))],
        out_specs=[pl.BlockSpec((tm, wd), lambda t: (t, 0)) for wd in widths],
        out_shape=[jax.ShapeDtypeStruct((rows, wd), F32) for wd in widths],
        compiler_params=_cparams(1),
        name="modlinear",
    )(x, mods, ln_g, w)


def _ssd_kernel(*refs, seq_len, has_h0, emit_state):
    it = iter(refs)
    x_ref, b_ref, c_ref, z_ref, dt_ref = (next(it) for _ in range(5))
    cwx_ref, cwb_ref, cwc_ref, cbx_ref, cbb_ref, cbc_ref = (next(it) for _ in range(6))
    dtb_ref, alog_ref, dsk_ref, ng_ref = (next(it) for _ in range(4))
    h0_ref = next(it) if has_h0 else None
    y_ref = next(it)
    st_ref = next(it) if emit_state else None
    xs_s, xt_s, b_s, c_s, acum_s, a8_s, dt8_s, yt_s, h_s = (next(it) for _ in range(9))
    n_chunks = seq_len // M_CHUNK

    def conv_silu(ref, w_ref, bias_ref):
        v = ref[...]
        t = lax.broadcasted_iota(jnp.int32, v.shape, 0)
        prev = jnp.where(t == 0, 0.0, pltpu.roll(v, 1, 0))
        nxt = jnp.where(t == seq_len - 1, 0.0, pltpu.roll(v, seq_len - 1, 0))
        w = w_ref[...]
        return _silu(prev * w[0:1, :] + v * w[1:2, :] + nxt * w[2:3, :] + bias_ref[...])

    xs = conv_silu(x_ref, cwx_ref, cbx_ref)
    xs_s[...] = xs
    b_s[...] = conv_silu(b_ref, cwb_ref, cbb_ref).astype(BF16)
    c_s[...] = conv_silu(c_ref, cwc_ref, cbc_ref).astype(BF16)
    dt = _softplus(dt_ref[...] + dtb_ref[...])
    dta = dt * (-jnp.exp(alog_ref[...]))
    row = lax.broadcasted_iota(jnp.int32, (M_CHUNK, M_CHUNK), 0)
    col = lax.broadcasted_iota(jnp.int32, (M_CHUNK, M_CHUNK), 1)
    keeps = (col <= row, col >= row)
    tris = [jnp.where(k, 1.0, 0.0).astype(BF16) for k in keeps]
    for c in range(n_chunks):
        r = slice(c * M_CHUNK, (c + 1) * M_CHUNK)
        dt_t = dt[r, :].T
        by_time = _split3(dta[r, :])
        by_head = _split3(dta[r, :].T)
        for d in range(2):
            hl = slice(M_GROUP_HEADS * d, M_GROUP_HEADS * (d + 1))
            acum_s[d, r, :] = functools.reduce(jnp.add, [_dot(tris[d], p) for p in by_time])
            acum_t = functools.reduce(jnp.add, [_dot_nt(p, tris[d]) for p in by_head])
            a8_s[d, c] = acum_t[hl, :]
            dt8_s[d, c] = dt_t[hl, :]
        for k in range(M_GROUP_CH // LANES):
            xt_s[c, k * LANES:(k + 1) * LANES, :] = xs[r, k * LANES:(k + 1) * LANES].T
    yt_s[...] = jnp.zeros(yt_s.shape, F32)
    if has_h0:
        h_s[...] = h0_ref[...]
    else:
        h_s[...] = jnp.zeros(h_s.shape, F32)

    def head_rows(v8):
        return jnp.concatenate(
            [jnp.broadcast_to(v8[r:r + 1, :], (M_HEADDIM, v8.shape[1]))
             for r in range(M_GROUP_HEADS)], axis=0)

    def chunk_step(c, d):
        start = c * M_CHUNK
        rows = pl.ds(start if isinstance(c, int) else pl.multiple_of(start, M_CHUNK), M_CHUNK)
        keep = keeps[d]
        end = M_CHUNK - 1 if d == 0 else 0
        acum = acum_s[d, rows, :]
        a8 = a8_s[d, c]
        dt8 = dt8_s[d, c]
        a_end8 = jnp.broadcast_to(a8[:, end:end + 1], a8.shape)
        w8 = jnp.exp(a_end8 - a8) * dt8
        bc = b_s[rows, :]
        cc = c_s[rows, :]
        cb = _dot_nt(cc, bc)
        h_in = h_s[d]
        h_s[d] = (h_in * head_rows(jnp.exp(a_end8))
                  + _dot((xt_s[c] * head_rows(w8)).astype(BF16), bc))
        yt_s[c] += _dot_nt(h_in.astype(BF16), cc) * head_rows(jnp.exp(a8))
        for r in range(M_GROUP_HEADS):
            ln = M_GROUP_HEADS * d + r
            hs = slice(r * M_HEADDIM, (r + 1) * M_HEADDIM)
            seg = jnp.broadcast_to(acum[:, ln:ln + 1], (M_CHUNK, M_CHUNK)) - a8[r:r + 1, :]
            decay = jnp.exp(jnp.where(keep, seg, -jnp.inf))
            m = (cb * decay * dt8[r:r + 1, :]).astype(BF16)
            yt_s[c, hs, :] += _dot_nt(xt_s[c, hs, :].astype(BF16), m)

    def both_directions(i):
        chunk_step(i, 0)
        chunk_step(n_chunks - 1 - i, 1)

    if n_chunks <= 2:
        for i in range(n_chunks):
            both_directions(i)
    else:
        def body(i, carry):
            both_directions(i)
            return carry
        lax.fori_loop(0, n_chunks, body, 0)

    for c in range(n_chunks):
        r = slice(c * M_CHUNK, (c + 1) * M_CHUNK)
        for k in range(M_GROUP_CH // LANES):
            cs = slice(k * LANES, (k + 1) * LANES)
            xs_s[r, cs] = xs_s[r, cs] * dsk_ref[:, cs] + yt_s[c, cs, :].T
    y = xs_s[...] * _silu(z_ref[...])
    y = y * lax.rsqrt(jnp.mean(y * y, axis=-1, keepdims=True) + EPS) * ng_ref[...]
    y_ref[...] = y.astype(BF16)
    if emit_state:
        st_ref[...] = h_s[...]


def _ssd(xbc, z, dt, conv_w, conv_b, dtb, alog, dsk, norm_g, h0, j, seq_len, emit_state):
    rows = xbc.shape[0]
    n_seq = rows // seq_len
    gx = M_INNER // LANES
    gc = gx + M_GROUPS * M_STATE // LANES
    has_h0 = h0 is not None
    n_chunks = seq_len // M_CHUNK
    st_block = (None, 2, M_GROUP_CH, M_STATE)

    in_specs = [pl.BlockSpec((seq_len, M_GROUP_CH), lambda s, g: (s, g)),
                pl.BlockSpec((seq_len, M_STATE), lambda s, g: (s, gx + g)),
                pl.BlockSpec((seq_len, M_STATE), lambda s, g: (s, gc + g)),
                pl.BlockSpec((seq_len, M_GROUP_CH), lambda s, g: (s, g)),
                pl.BlockSpec((seq_len, LANES), lambda s, g: (s, g)),
                pl.BlockSpec((None, M_CONV, M_GROUP_CH), lambda s, g: (j, 0, g)),
                pl.BlockSpec((None, M_CONV, M_STATE), lambda s, g: (j, 0, gx + g)),
                pl.BlockSpec((None, M_CONV, M_STATE), lambda s, g: (j, 0, gc + g)),
                pl.BlockSpec((None, 1, M_GROUP_CH), lambda s, g: (j, 0, g)),
                pl.BlockSpec((None, 1, M_STATE), lambda s, g: (j, 0, gx + g)),
                pl.BlockSpec((None, 1, M_STATE), lambda s, g: (j, 0, gc + g)),
                pl.BlockSpec((None, 1, LANES), lambda s, g: (j, 0, g)),
                pl.BlockSpec((None, 1, LANES), lambda s, g: (j, 0, g)),
                pl.BlockSpec((None, 1, M_GROUP_CH), lambda s, g: (j, 0, g)),
                pl.BlockSpec((None, 1, M_GROUP_CH), lambda s, g: (j, 0, g))]
    args = [xbc, xbc, xbc, z, dt, conv_w, conv_w, conv_w, conv_b, conv_b, conv_b,
            dtb, alog, dsk, norm_g]
    if has_h0:
        in_specs.append(pl.BlockSpec(st_block, lambda s, g: (s, 0, g, 0)))
        args.append(h0)
    out_specs = [pl.BlockSpec((seq_len, M_GROUP_CH), lambda s, g: (s, g))]
    out_shape = [jax.ShapeDtypeStruct((rows, M_INNER), BF16)]
    if emit_state:
        out_specs.append(pl.BlockSpec(st_block, lambda s, g: (s, 0, g, 0)))
        out_shape.append(jax.ShapeDtypeStruct((n_seq, 2, M_INNER, M_STATE), F32))
    scratch = [pltpu.VMEM((seq_len, M_GROUP_CH), F32),
               pltpu.VMEM((n_chunks, M_GROUP_CH, M_CHUNK), F32),
               pltpu.VMEM((seq_len, M_STATE), BF16),
               pltpu.VMEM((seq_len, M_STATE), BF16),
               pltpu.VMEM((2, seq_len, LANES), F32),
               pltpu.VMEM((2, n_chunks, M_GROUP_HEADS, M_CHUNK), F32),
               pltpu.VMEM((2, n_chunks, M_GROUP_HEADS, M_CHUNK), F32),
               pltpu.VMEM((n_chunks, M_GROUP_CH, M_CHUNK), F32),
               pltpu.VMEM((2, M_GROUP_CH, M_STATE), F32)]
    return pl.pallas_call(
        functools.partial(_ssd_kernel, seq_len=seq_len, has_h0=has_h0, emit_state=emit_state),
        grid=(n_seq, M_GROUPS),
        in_specs=in_specs,
        out_specs=out_specs,
        out_shape=out_shape,
        scratch_shapes=scratch,
        compiler_params=_cparams(2),
        name="ssd",
    )(*args)


def _gmlp_kernel(x_ref, mod_ref, g_ref, win_ref, bin_ref, ng_ref, ws_ref, bs_ref, wout_ref,
                 o_ref, gate_s, *, tm):
    x = x_ref[...]
    h = _modnorm(x, g_ref[1:2, :], mod_ref[3:4, :], mod_ref[4:5, :]).astype(BF16)
    hg = jax.nn.gelu(_dot(h, win_ref[...]) + bin_ref[...])
    u = hg[:, :G_INNER]
    v = hg[:, G_INNER:]
    v = v * lax.rsqrt(jnp.mean(v * v, axis=-1, keepdims=True) + EPS) * ng_ref[...]
    vb = v.astype(BF16)
    bs = bs_ref[...]
    for c in range(tm // G_CHUNK):
        r = slice(c * G_CHUNK, (c + 1) * G_CHUNK)
        for hd in range(G_HEADS):
            cs = slice(hd * G_HEAD_CH, (hd + 1) * G_HEAD_CH)
            sv = _dot(ws_ref[hd], vb[r, cs]) + bs[:, hd:hd + 1]
            gate_s[r, cs] = (u[r, cs] * sv).astype(BF16)
    o_ref[...] = x + mod_ref[5:6, :] * _dot(gate_s[...], wout_ref[...])


def _gmlp(x, mods, ln_g, w_in, b_in, norm_g, w_s, b_s_t, w_out, layer, j, pop):
    tm = 512
    rows = x.shape[0]
    return pl.pallas_call(
        functools.partial(_gmlp_kernel, tm=tm),
        grid=(rows // tm,),
        in_specs=[pl.BlockSpec((tm, D_MODEL), lambda t: (t, 0)),
                  _mod_spec(layer, pop, tm),
                  _ln_spec(layer),
                  _resident((None, D_MODEL, 2 * G_INNER), lambda t: (j, 0, 0)),
                  _resident((None, 1, 2 * G_INNER), lambda t: (j, 0, 0)),
                  _resident((None, 1, G_INNER), lambda t: (j, 0, 0)),
                  _resident((None, G_HEADS, G_CHUNK, G_CHUNK), lambda t: (j, 0, 0, 0)),
                  _resident((None, G_CHUNK, G_HEADS), lambda t: (j, 0, 0)),
                  _resident((None, G_INNER, D_MODEL), lambda t: (j, 0, 0))],
        out_specs=pl.BlockSpec((tm, D_MODEL), lambda t: (t, 0)),
        out_shape=jax.ShapeDtypeStruct((rows, D_MODEL), F32),
        scratch_shapes=[pltpu.VMEM((tm, G_INNER), BF16)],
        compiler_params=_cparams(1),
        name="gmlp",
    )(x, mods, ln_g, w_in, b_in, norm_g, w_s, b_s_t, w_out)


def _dft_tables(seq_len):
    def angles(n):
        k = np.arange(n, dtype=np.int64)
        return 2.0 * np.pi * ((k[:, None] * k[None, :]) % n).astype(np.float64) / n
    ac = angles(F_GROUP_CH)
    al = angles(seq_len)
    chan = np.concatenate([np.cos(ac), np.sin(ac)], axis=1).astype(np.float32)
    pos = np.concatenate([np.cos(al), -np.sin(al)], axis=1).astype(np.float32)
    return chan, pos


def _fnet_kernel(x_ref, mod_ref, g_ref, chan_ref, pos_ref, wout_ref, b_ref, o_ref, f_s,
                 *, seq_len):
    x = x_ref[...]
    h = _modnorm(x, g_ref[1:2, :], mod_ref[3:4, :], mod_ref[4:5, :]).astype(BF16)
    chan = chan_ref[...].astype(BF16)
    pos = pos_ref[...].astype(BF16)
    scale = 1.0 / math.sqrt(seq_len * F_GROUP_CH)
    for g in range(F_GROUPS):
        cs = slice(g * F_GROUP_CH, (g + 1) * F_GROUP_CH)
        p = _dot(h[:, cs], chan)
        stacked = jnp.concatenate([p[:, :F_GROUP_CH], p[:, F_GROUP_CH:]], axis=0).astype(BF16)
        f_s[:, cs] = (_dot(pos, stacked) * scale).astype(BF16)
    o_ref[...] = x + mod_ref[5:6, :] * (_dot(f_s[...], wout_ref[...]) + b_ref[...])


def _fnet(x, mods, ln_g, w_out, b_out, layer, j, pop, seq_len):
    rows = x.shape[0]
    chan, pos = _dft_tables(seq_len)
    return pl.pallas_call(
        functools.partial(_fnet_kernel, seq_len=seq_len),
        grid=(rows // seq_len,),
        in_specs=[pl.BlockSpec((seq_len, D_MODEL), lambda t: (t, 0)),
                  _mod_spec(layer, pop, seq_len),
                  _ln_spec(layer),
                  _resident(chan.shape, lambda t: (0, 0)),
                  _resident(pos.shape, lambda t: (0, 0)),
                  _resident((None, D_MODEL, D_MODEL), lambda t: (j, 0, 0)),
                  _resident((None, 1, D_MODEL), lambda t: (j, 0, 0))],
        out_specs=pl.BlockSpec((seq_len, D_MODEL), lambda t: (t, 0)),
        out_shape=jax.ShapeDtypeStruct((rows, D_MODEL), F32),
        scratch_shapes=[pltpu.VMEM((seq_len, D_MODEL), BF16)],
        compiler_params=_cparams(1),
        name="fnet",
    )(x, mods, ln_g, jnp.asarray(chan), jnp.asarray(pos), w_out, b_out)


def _head_mean_matrix():
    i = np.arange(2 * LANES)
    return ((i[:, None] // A_HD) == (i[None, :] // A_HD)).astype(np.float32) / A_HD


def _rope_tables():
    half = A_HD // 4
    inv = (np.float32(ROPE_THETA) ** (-np.arange(half, dtype=np.float32) / np.float32(half)))
    inv = inv.astype(np.float32)
    t = np.arange(DEC_SEQ)
    lane = np.arange(LANES)
    hl = lane % A_HD
    posn = np.where(hl[None, :] < A_HD // 2, (t // GRID_W)[:, None], (t % GRID_W)[:, None])
    sub = hl % (A_HD // 2)
    ang = posn.astype(np.float32) * inv[sub % half][None, :]
    cos = np.cos(ang).astype(np.float32)
    sin = np.sin(ang).astype(np.float32)
    lower = (sub < half)[None, :]
    sin_up = np.where(lower, -sin, 0.0).astype(np.float32)
    sin_dn = np.where(lower, 0.0, sin).astype(np.float32)
    return cos, sin_up, sin_dn


def _qkv_kernel(*refs, rope):
    it = iter(refs)
    x_ref, mod_ref, g_ref, w_ref, qg_ref, kg_ref, avg_ref = (next(it) for _ in range(7))
    if rope:
        cos_ref, sup_ref, sdn_ref = (next(it) for _ in range(3))
    q_ref, k_ref, v_ref = (next(it) for _ in range(3))
    h = _modnorm(x_ref[...], g_ref[1:2, :], mod_ref[3:4, :], mod_ref[4:5, :]).astype(BF16)
    qkv = _dot(h, w_ref[...])
    avg = avg_ref[...]
    half = A_HD // 4

    def head_norm(t, gain):
        hi, lo = _split2(t * t)
        ms = _dot(hi, avg) + _dot(lo, avg)
        return t * lax.rsqrt(ms + EPS) * gain

    def rotate(t):
        up = pltpu.roll(t, LANES - half, 1)
        dn = pltpu.roll(t, half, 1)
        return t * cos_ref[...] + up * sup_ref[...] + dn * sdn_ref[...]

    slab = 2 * LANES
    for s in range((A_Q + A_KVW) // slab):
        cs = slice(s * slab, (s + 1) * slab)
        is_q = s < A_Q // slab
        gain = qg_ref[:, cs] if is_q else kg_ref[...]
        t = head_norm(qkv[:, cs], gain)
        if rope:
            t = jnp.concatenate([rotate(t[:, :LANES]), rotate(t[:, LANES:])], axis=1)
        if is_q:
            q_ref[:, cs] = t.astype(q_ref.dtype)
        else:
            k_ref[...] = t.astype(k_ref.dtype)
    v_ref[...] = qkv[:, A_Q + A_KVW:].astype(v_ref.dtype)


def _qkv(x, mods, ln_g, w, q_gain, k_gain, layer, j, pop):
    tm = 512
    rows = x.shape[0]
    rope = pop == 1
    kv_dtype = BF16 if rope else F32
    in_specs = [pl.BlockSpec((tm, D_MODEL), lambda t: (t, 0)),
                _mod_spec(layer, pop, tm),
                _ln_spec(layer),
                _resident((None, D_MODEL, A_Q + 2 * A_KVW), lambda t: (j, 0, 0)),
                _resident((None, 1, A_Q), lambda t: (j, 0, 0)),
                _resident((None, 1, A_KVW), lambda t: (j, 0, 0)),
                _resident((2 * LANES, 2 * LANES), lambda t: (0, 0))]
    args = [x, mods, ln_g, w, q_gain, k_gain, jnp.asarray(_head_mean_matrix(), BF16)]
    if rope:
        per_seq = DEC_SEQ // tm
        in_specs += [pl.BlockSpec((tm, LANES), lambda t: (t % per_seq, 0))] * 3
        args += [jnp.asarray(a) for a in _rope_tables()]
    return pl.pallas_call(
        functools.partial(_qkv_kernel, rope=rope),
        grid=(rows // tm,),
        in_specs=in_specs,
        out_specs=[pl.BlockSpec((tm, A_Q), lambda t: (t, 0)),
                   pl.BlockSpec((tm, A_KVW), lambda t: (t, 0)),
                   pl.BlockSpec((tm, A_KVW), lambda t: (t, 0))],
        out_shape=[jax.ShapeDtypeStruct((rows, A_Q), BF16),
                   jax.ShapeDtypeStruct((rows, A_KVW), kv_dtype),
                   jax.ShapeDtypeStruct((rows, A_KVW), kv_dtype)],
        compiler_params=_cparams(1),
        name="qkv",
    )(*args)


def _head_place_matrices():
    m = np.zeros((A_KV, A_KVW, A_GRP * A_HD), np.float32)
    d = np.arange(A_HD)
    for j in range(A_KV):
        for g in range(A_GRP):
            m[j, j * A_HD + d, g * A_HD + d] = 1.0
    return m


def _attn_kernel(q_ref, place_ref, *refs, n_src):
    kv_refs, o_ref = refs[:2 * n_src], refs[2 * n_src]
    lane = lax.broadcasted_iota(jnp.int32, (1, A_GRP * A_HD), 1)
    scale = A_HD ** -0.5
    for j in range(A_KV):
        place = place_ref[j]
        qs = q_ref[:, j * A_GRP * A_HD:(j + 1) * A_GRP * A_HD]
        k4 = [_dot(kv_refs[2 * s][...].astype(BF16), place).astype(BF16) for s in range(n_src)]
        v4 = [_dot(kv_refs[2 * s + 1][...].astype(BF16), place).astype(BF16) for s in range(n_src)]
        acc = None
        for g in range(A_GRP):
            own = (lane >= g * A_HD) & (lane < (g + 1) * A_HD)
            qm = jnp.where(own, qs, jnp.zeros_like(qs))
            sc = [_dot_nt(qm, k4[s]) * scale for s in range(n_src)]
            mx = functools.reduce(jnp.maximum, [jnp.max(t, axis=-1, keepdims=True) for t in sc])
            p = [jnp.exp(t - mx) for t in sc]
            den = functools.reduce(jnp.add, [jnp.sum(t, axis=-1, keepdims=True) for t in p])
            pv = functools.reduce(jnp.add, [
                _dot(p[s].astype(BF16), jnp.where(own, v4[s], jnp.zeros_like(v4[s])))
                for s in range(n_src)])
            acc = pv / den if acc is None else acc + pv / den
        o_ref[:, j * A_GRP * A_HD:(j + 1) * A_GRP * A_HD] = acc.astype(o_ref.dtype)


def _attn(q, sources, seq_len, tq):
    rows = q.shape[0]
    n_batch = rows // seq_len
    per = seq_len // tq
    in_specs = [pl.BlockSpec((tq, A_Q), lambda b, t: (b * per + t, 0)),
                _resident((A_KV, A_KVW, A_GRP * A_HD), lambda b, t: (0, 0, 0))]
    args = [q, jnp.asarray(_head_place_matrices(), BF16)]
    for k, v, lk in sources:
        in_specs += [pl.BlockSpec((lk, A_KVW), lambda b, t: (b, 0))] * 2
        args += [k, v]
    return pl.pallas_call(
        functools.partial(_attn_kernel, n_src=len(sources)),
        grid=(n_batch, per),
        in_specs=in_specs,
        out_specs=pl.BlockSpec((tq, A_Q), lambda b, t: (b * per + t, 0)),
        out_shape=jax.ShapeDtypeStruct((rows, A_Q), BF16),
        compiler_params=_cparams(2),
        name="attn",
    )(*args)


def _regroup_heads(a):
    lead = a.shape[:-1]
    a = a.reshape(lead + (2, M_GROUPS, M_GROUP_HEADS))
    a = jnp.swapaxes(a, -3, -2).reshape(lead + (M_GROUPS, 2 * M_GROUP_HEADS))
    pad = [(0, 0)] * (len(lead) + 1) + [(0, LANES - 2 * M_GROUP_HEADS)]
    return jnp.pad(a, pad).reshape(lead + (M_GROUPS * LANES,))


def kernel(x_prompt, x_sample, state_ssm, cache_k, cache_v, c, c_ctx,
           ln_g, ada_w, ada_b, ff1_w_in, ff1_w_out, ff2_w_in, ff2_w_out,
           m_w_in, m_conv_w, m_conv_b, m_dt_bias, m_a_log, m_d, m_norm_g, m_w_out,
           g_w_in, g_b_in, g_norm_g, g_w_s, g_b_s, g_w_out,
           f_w_out, f_b_out,
           a_w_qkv, a_q_norm, a_k_norm, a_w_o):
    xs = [x_prompt.reshape(TP, D_MODEL), x_sample.reshape(TS, D_MODEL)]
    seq = (SEQ, DEC_SEQ)

    conds = jnp.concatenate(
        [c_ctx[None], c, jnp.zeros((COND_ROWS - 1 - DEC_BATCH, D_MODEL), F32)], axis=0)
    mods = _adaln(conds, ada_w, ada_b)

    ff_w = {(0, 0): (ff1_w_in[0].astype(BF16), ff1_w_out[0].astype(BF16))}

    new_ssm, new_k, new_v = [], [], []
    for i in range(DEPTH):
        kind, j = i % N_MIXERS, i // N_MIXERS
        later = [(i, 1)] + ([(i + 1, 0)] if i + 1 < DEPTH else [])
        convert = [((ff2_w_in, ff2_w_out) if second else (ff1_w_in, ff1_w_out)) + (layer,)
                   for layer, second in later]
        w_in, w_out = ff_w[(i, 0)]
        xs[0], cast = _ffn(xs[0], mods, ln_g, w_in, w_out, i, False, 0, convert=convert)
        ff_w.update(zip(later, cast))
        xs[1], _ = _ffn(xs[1], mods, ln_g, w_in, w_out, i, False, 1)
        mix = [None, None]
        if kind == 0:
            n_zx = M_INNER + M_CONV_CH
            w_in = jnp.concatenate(
                [m_w_in[..., :n_zx], _regroup_heads(m_w_in[..., n_zx:])], axis=-1).astype(BF16)
            w_out = m_w_out.astype(BF16)
            dtb = _regroup_heads(m_dt_bias.reshape(-1, 1, 2 * M_HEADS))
            alog = _regroup_heads(m_a_log.reshape(-1, 1, 2 * M_HEADS))
            dsk = jnp.repeat(m_d, M_HEADDIM, axis=-1)[:, None, :]
            conv_b = m_conv_b[:, None, :]
            norm_g = m_norm_g[:, None, :]
            h0 = state_ssm[:, j].reshape(DEC_BATCH, 2, M_INNER, M_STATE)
            for p in range(2):
                z, xbc, dt = _modlinear(xs[p], mods, ln_g, w_in, i, j, p,
                                        (M_INNER, M_CONV_CH, M_GROUPS * LANES))
                res = _ssd(xbc, z, dt, m_conv_w, conv_b, dtb, alog, dsk, norm_g,
                           h0 if p == 1 else None, j, seq[p], emit_state=(p == 0))
                if p == 0:
                    new_ssm.append(res[1].reshape(BATCH, 2, M_HEADS, M_HEADDIM, M_STATE))
                mix[p] = (res[0], w_out, j)
        elif kind == 1:
            w_in, w_out, w_s = g_w_in.astype(BF16), g_w_out.astype(BF16), g_w_s.astype(BF16)
            b_s_t = jnp.swapaxes(g_b_s, -1, -2)
            xs = [_gmlp(xs[p], mods, ln_g, w_in, g_b_in[:, None, :], g_norm_g[:, None, :],
                        w_s, b_s_t, w_out, i, j, p) for p in range(2)]
        elif kind == 2:
            w_out = f_w_out.astype(BF16)
            xs = [_fnet(xs[p], mods, ln_g, w_out, f_b_out[:, None, :], i, j, p, seq[p])
                  for p in range(2)]
        else:
            w_qkv, w_o = a_w_qkv.astype(BF16), a_w_o.astype(BF16)
            q_gain = jnp.tile(a_q_norm, (1, A_HEADS))[:, None, :]
            k_gain = jnp.tile(a_k_norm, (1, A_KV))[:, None, :]
            qp, kp, vp = _qkv(xs[0], mods, ln_g, w_qkv, q_gain, k_gain, i, j, 0)
            new_k.append(kp.reshape(BATCH, SEQ, A_KV, A_HD))
            new_v.append(vp.reshape(BATCH, SEQ, A_KV, A_HD))
            op = _attn(qp, [(kp, vp, SEQ)], SEQ, SEQ)
            ql, kl, vl = _qkv(xs[1], mods, ln_g, w_qkv, q_gain, k_gain, i, j, 1)
            kc = cache_k[:, j].reshape(DEC_BATCH * PAST_LEN, A_KVW)
            vc = cache_v[:, j].reshape(DEC_BATCH * PAST_LEN, A_KVW)
            ol = _attn(ql, [(kc, vc, PAST_LEN), (kl, vl, DEC_SEQ)], DEC_SEQ, 256)
            mix = [(op, w_o, j), (ol, w_o, j)]
        w_in, w_out = ff_w[(i, 1)]
        xs = [_ffn(xs[p], mods, ln_g, w_in, w_out, i, True, p, mix[p])[0] for p in range(2)]

    return (xs[0].reshape(BATCH, SEQ, D_MODEL),
            xs[1].reshape(DEC_BATCH, DEC_SEQ, D_MODEL),
            jnp.stack(new_ssm, axis=1),
            jnp.stack(new_k, axis=1),
            jnp.stack(new_v, axis=1))
```

```python
import functools
import math

import numpy as np
import jax
import jax.numpy as jnp
from jax import lax
from jax.experimental import pallas as pl
from jax.experimental.pallas import tpu as pltpu

F32 = jnp.float32
BF16 = jnp.bfloat16

D_MODEL = 1024
BATCH = 32
SEQ = 256
DEPTH = 4
DEC_BATCH = 2
DEC_SEQ = 1024
PAST_LEN = 256
GRID_W = 64
N_MIXERS = 4
N_MOD = 9
EPS = 1e-6
FFN_HIDDEN = 2816

M_INNER = 2 * D_MODEL
M_HEADDIM = 64
M_HEADS = M_INNER // M_HEADDIM
M_GROUPS = 4
M_STATE = 128
M_CONV = 3
M_CHUNK = 128
M_CONV_CH = M_INNER + 2 * M_GROUPS * M_STATE
M_GROUP_CH = M_INNER // M_GROUPS
M_GROUP_HEADS = M_HEADS // M_GROUPS

G_CHUNK = 128
G_INNER = 2 * D_MODEL
G_HEADS = 8
G_HEAD_CH = G_INNER // G_HEADS

F_GROUPS = 4
F_GROUP_CH = D_MODEL // F_GROUPS

A_HEADS = 16
A_KV = 4
A_HD = 64
A_GRP = A_HEADS // A_KV
A_Q = A_HEADS * A_HD
A_KVW = A_KV * A_HD
ROPE_THETA = 10000.0

LANES = 128
COND_ROWS = 8
TP = BATCH * SEQ
TS = DEC_BATCH * DEC_SEQ
VMEM_LIMIT = 56 * 2**20


def _cparams(n_grid):
    return pltpu.CompilerParams(dimension_semantics=("arbitrary",) * n_grid,
                                vmem_limit_bytes=VMEM_LIMIT)


def _resident(block_shape, index_map):
    return pl.BlockSpec(block_shape, index_map, pipeline_mode=pl.Buffered(1))


def _dot(a, b):
    return jnp.dot(a, b, preferred_element_type=F32)


def _dot_nt(a, b):
    return lax.dot_general(a, b, (((1,), (1,)), ((), ())), preferred_element_type=F32)


def _silu(x):
    return x * jax.nn.sigmoid(x)


def _softplus(x):
    return jnp.maximum(x, 0.0) + jnp.log1p(jnp.exp(-jnp.abs(x)))


def _modnorm(x, g, shift, scale):
    y = x * lax.rsqrt(jnp.mean(x * x, axis=-1, keepdims=True) + EPS) * g
    return y * (1.0 + scale) + shift


def _split3(v):
    hi = v.astype(BF16)
    r1 = v - hi.astype(F32)
    mid = r1.astype(BF16)
    lo = (r1 - mid.astype(F32)).astype(BF16)
    return hi, mid, lo


def _split2(v):
    hi = v.astype(BF16)
    return hi, (v - hi.astype(F32)).astype(BF16)


def _adaln_kernel(c_ref, w_ref, b_ref, o_ref):
    s = _silu(c_ref[...]).astype(BF16)
    o_ref[...] = _dot(s, w_ref[...].astype(BF16)) + b_ref[...]


def _adaln(conds, ada_w, ada_b):
    tn = 2304
    n = N_MOD * D_MODEL
    out = pl.pallas_call(
        _adaln_kernel,
        grid=(DEPTH, n // tn),
        in_specs=[pl.BlockSpec((COND_ROWS, D_MODEL), lambda i, j: (0, 0)),
                  pl.BlockSpec((None, D_MODEL, tn), lambda i, j: (i, 0, j)),
                  pl.BlockSpec((None, 1, tn), lambda i, j: (i, 0, j))],
        out_specs=pl.BlockSpec((None, COND_ROWS, tn), lambda i, j: (i, 0, j)),
        out_shape=jax.ShapeDtypeStruct((DEPTH, COND_ROWS, n), F32),
        compiler_params=_cparams(2),
        name="adaln",
    )(conds, ada_w, ada_b.reshape(DEPTH, 1, n))
    return out.reshape(DEPTH, COND_ROWS, N_MOD, D_MODEL)


def _mod_spec(layer, pop, tm):
    if pop == 0:
        return pl.BlockSpec((None, None, N_MOD, D_MODEL), lambda t, *_: (layer, 0, 0, 0))
    return pl.BlockSpec((None, None, N_MOD, D_MODEL),
                        lambda t, *_: (layer, 1 + (t * tm) // DEC_SEQ, 0, 0))


def _ln_spec(layer):
    return pl.BlockSpec((None, 3, D_MODEL), lambda t, *_: (layer, 0, 0))


MXU_COLS = 256
FFN_PIECES = tuple((lo, min(lo + 3 * MXU_COLS, FFN_HIDDEN))
                   for lo in range(0, FFN_HIDDEN, 3 * MXU_COLS))


def _ffn_kernel(*refs, k0, gk, has_mix, n_convert):
    it = iter(refs)
    if has_mix:
        y_ref, wmix_ref = next(it), next(it)
    x_ref, mod_ref, g_ref, win_ref, wout_ref = (next(it) for _ in range(5))
    src_refs = [next(it) for _ in range(n_convert)]
    o_ref = next(it)
    dst_refs = [next(it) for _ in range(n_convert)]

    x = x_ref[...]
    if has_mix:
        x = x + mod_ref[5:6, :] * _dot(y_ref[...], wmix_ref[...])
    h = _modnorm(x, g_ref[gk:gk + 1, :], mod_ref[k0:k0 + 1, :], mod_ref[k0 + 1:k0 + 2, :])
    h = h.astype(BF16)
    o = None
    for lo, hi in FFN_PIECES:
        gate = _dot(h, win_ref[:, lo:hi])
        up = _dot(h, win_ref[:, FFN_HIDDEN + lo:FFN_HIDDEN + hi])
        part = _dot((_silu(gate) * up).astype(BF16), wout_ref[lo:hi, :])
        o = part if o is None else o + part
    o_ref[...] = x + 0.5 * mod_ref[k0 + 2:k0 + 3, :] * o
    for src, dst in zip(src_refs, dst_refs):
        dst[...] = src[...].astype(BF16)


def _ffn(x, mods, ln_g, w_in, w_out, layer, second, pop, mix=None, convert=()):
    tm = 512
    rows = x.shape[0]
    steps = rows // tm
    in_specs = [pl.BlockSpec((tm, D_MODEL), lambda t: (t, 0)),
                _mod_spec(layer, pop, tm),
                _ln_spec(layer),
                _resident((D_MODEL, 2 * FFN_HIDDEN), lambda t: (0, 0)),
                _resident((FFN_HIDDEN, D_MODEL), lambda t: (0, 0))]
    args = [x, mods, ln_g, w_in, w_out]
    if mix is not None:
        y, w_mix = mix
        k = y.shape[1]
        in_specs = [pl.BlockSpec((tm, k), lambda t: (t, 0)),
                    _resident((k, D_MODEL), lambda t: (0, 0))] + in_specs
        args = [y, w_mix] + args
    out_specs = [pl.BlockSpec((tm, D_MODEL), lambda t: (t, 0))]
    out_shape = [jax.ShapeDtypeStruct((rows, D_MODEL), F32)]
    for src, index, n_cols in convert:
        n_rows = src.shape[1]
        slab = n_rows // steps
        in_specs.append(pl.BlockSpec((None, slab, n_cols), lambda t, index=index: (index, t, 0)))
        args.append(src)
        out_specs.append(pl.BlockSpec((slab, n_cols), lambda t: (t, 0)))
        out_shape.append(jax.ShapeDtypeStruct((n_rows, n_cols), BF16))
    res = pl.pallas_call(
        functools.partial(_ffn_kernel, k0=6 if second else 0, gk=2 if second else 0,
                          has_mix=mix is not None, n_convert=len(convert)),
        grid=(steps,),
        in_specs=in_specs,
        out_specs=out_specs,
        out_shape=out_shape,
        compiler_params=_cparams(1),
        name="ffn",
    )(*args)
    return res[0], list(res[1:])


def _modlinear_kernel(x_ref, mod_ref, g_ref, *refs, widths):
    w_refs, o_refs = refs[:len(widths)], iter(refs[len(widths):])
    h = _modnorm(x_ref[...], g_ref[1:2, :], mod_ref[3:4, :], mod_ref[4:5, :]).astype(BF16)
    for w_ref, group in zip(w_refs, widths):
        off = 0
        for wd in group:
            next(o_refs)[...] = _dot(h, w_ref[:, off:off + wd])
            off += wd


def _modlinear(x, mods, ln_g, weights, layer, pop, widths):
    tm = 512
    rows = x.shape[0]
    flat = [wd for group in widths for wd in group]
    return pl.pallas_call(
        functools.partial(_modlinear_kernel, widths=widths),
        grid=(rows // tm,),
        in_specs=[pl.BlockSpec((tm, D_MODEL), lambda t: (t, 0)),
                  _mod_spec(layer, pop, tm),
                  _ln_spec(layer)]
                 + [_resident(w.shape, lambda t: (0, 0)) for w in weights],
        out_specs=[pl.BlockSpec((tm, wd), lambda t: (t, 0)) for wd in flat],
        out_shape=[jax.ShapeDtypeStruct((rows, wd), F32) for wd in flat],
        compiler_params=_cparams(1),
        name="modlinear",
    )(x, mods, ln_g, *weights)


def _ssd_kernel(*refs, seq_len, has_h0, emit_state):
    it = iter(refs)
    x_ref, b_ref, c_ref, z_ref, dt_ref = (next(it) for _ in range(5))
    cwx_ref, cwb_ref, cwc_ref, cbx_ref, cbb_ref, cbc_ref = (next(it) for _ in range(6))
    dtb_ref, alog_ref, dsk_ref, ng_ref = (next(it) for _ in range(4))
    h0_ref = next(it) if has_h0 else None
    y_ref = next(it)
    st_ref = next(it) if emit_state else None
    xs_s, xt_s, b_s, c_s, acum_s, a8_s, dt8_s, yt_s, h_s = (next(it) for _ in range(9))
    n_chunks = seq_len // M_CHUNK

    def conv_silu(ref, w_ref, bias_ref):
        v = ref[...]
        t = lax.broadcasted_iota(jnp.int32, v.shape, 0)
        prev = jnp.where(t == 0, 0.0, pltpu.roll(v, 1, 0))
        nxt = jnp.where(t == seq_len - 1, 0.0, pltpu.roll(v, seq_len - 1, 0))
        w = w_ref[...]
        return _silu(prev * w[0:1, :] + v * w[1:2, :] + nxt * w[2:3, :] + bias_ref[...])

    xs = conv_silu(x_ref, cwx_ref, cbx_ref)
    xs_s[...] = xs
    b_s[...] = conv_silu(b_ref, cwb_ref, cbb_ref).astype(BF16)
    c_s[...] = conv_silu(c_ref, cwc_ref, cbc_ref).astype(BF16)
    dt = _softplus(dt_ref[...] + dtb_ref[...])
    dta = dt * (-jnp.exp(alog_ref[...]))
    row = lax.broadcasted_iota(jnp.int32, (M_CHUNK, M_CHUNK), 0)
    col = lax.broadcasted_iota(jnp.int32, (M_CHUNK, M_CHUNK), 1)
    keeps = (col <= row, col >= row)
    tris = [jnp.where(k, 1.0, 0.0).astype(BF16) for k in keeps]
    for c in range(n_chunks):
        r = slice(c * M_CHUNK, (c + 1) * M_CHUNK)
        dt_t = dt[r, :].T
        by_time = _split3(dta[r, :])
        by_head = _split3(dta[r, :].T)
        for d in range(2):
            hl = slice(M_GROUP_HEADS * d, M_GROUP_HEADS * (d + 1))
            acum_s[d, r, :] = functools.reduce(jnp.add, [_dot(tris[d], p) for p in by_time])
            acum_t = functools.reduce(jnp.add, [_dot_nt(p, tris[d]) for p in by_head])
            a8_s[d, c] = acum_t[hl, :]
            dt8_s[d, c] = dt_t[hl, :]
        for k in range(M_GROUP_CH // LANES):
            xt_s[c, k * LANES:(k + 1) * LANES, :] = xs[r, k * LANES:(k + 1) * LANES].T
    yt_s[...] = jnp.zeros(yt_s.shape, F32)
    if has_h0:
        h_s[...] = h0_ref[...]
    else:
        h_s[...] = jnp.zeros(h_s.shape, F32)

    def head_rows(v8):
        return jnp.concatenate(
            [jnp.broadcast_to(v8[r:r + 1, :], (M_HEADDIM, v8.shape[1]))
             for r in range(M_GROUP_HEADS)], axis=0)

    def chunk_step(c, d):
        start = c * M_CHUNK
        rows = pl.ds(start if isinstance(c, int) else pl.multiple_of(start, M_CHUNK), M_CHUNK)
        keep = keeps[d]
        end = M_CHUNK - 1 if d == 0 else 0
        acum = acum_s[d, rows, :]
        a8 = a8_s[d, c]
        dt8 = dt8_s[d, c]
        a_end8 = jnp.broadcast_to(a8[:, end:end + 1], a8.shape)
        w8 = jnp.exp(a_end8 - a8) * dt8
        bc = b_s[rows, :]
        cc = c_s[rows, :]
        cb = _dot_nt(cc, bc)
        h_in = h_s[d]
        h_s[d] = (h_in * head_rows(jnp.exp(a_end8))
                  + _dot((xt_s[c] * head_rows(w8)).astype(BF16), bc))
        yt_s[c] += _dot_nt(h_in.astype(BF16), cc) * head_rows(jnp.exp(a8))
        for r in range(M_GROUP_HEADS):
            ln = M_GROUP_HEADS * d + r
            hs = slice(r * M_HEADDIM, (r + 1) * M_HEADDIM)
            seg = jnp.broadcast_to(acum[:, ln:ln + 1], (M_CHUNK, M_CHUNK)) - a8[r:r + 1, :]
            decay = jnp.exp(jnp.where(keep, seg, -jnp.inf))
            m = (cb * decay * dt8[r:r + 1, :]).astype(BF16)
            yt_s[c, hs, :] += _dot_nt(xt_s[c, hs, :].astype(BF16), m)

    def both_directions(i):
        chunk_step(i, 0)
        chunk_step(n_chunks - 1 - i, 1)

    if n_chunks <= 2:
        for i in range(n_chunks):
            both_directions(i)
    else:
        def body(i, carry):
            both_directions(i)
            return carry
        lax.fori_loop(0, n_chunks, body, 0)

    for c in range(n_chunks):
        r = slice(c * M_CHUNK, (c + 1) * M_CHUNK)
        for k in range(M_GROUP_CH // LANES):
            cs = slice(k * LANES, (k + 1) * LANES)
            xs_s[r, cs] = xs_s[r, cs] * dsk_ref[:, cs] + yt_s[c, cs, :].T
    y = xs_s[...] * _silu(z_ref[...])
    y = y * lax.rsqrt(jnp.mean(y * y, axis=-1, keepdims=True) + EPS) * ng_ref[...]
    y_ref[...] = y.astype(BF16)
    if emit_state:
        st_ref[...] = h_s[...]


def _ssd(xbc, z, dt, conv_w, conv_b, dtb, alog, dsk, norm_g, h0, j, seq_len, emit_state):
    rows = xbc.shape[0]
    n_seq = rows // seq_len
    gx = M_INNER // LANES
    gc = gx + M_GROUPS * M_STATE // LANES
    has_h0 = h0 is not None
    n_chunks = seq_len // M_CHUNK
    st_block = (None, 2, M_GROUP_CH, M_STATE)

    in_specs = [pl.BlockSpec((seq_len, M_GROUP_CH), lambda s, g: (s, g)),
                pl.BlockSpec((seq_len, M_STATE), lambda s, g: (s, gx + g)),
                pl.BlockSpec((seq_len, M_STATE), lambda s, g: (s, gc + g)),
                pl.BlockSpec((seq_len, M_GROUP_CH), lambda s, g: (s, g)),
                pl.BlockSpec((seq_len, LANES), lambda s, g: (s, g)),
                pl.BlockSpec((None, M_CONV, M_GROUP_CH), lambda s, g: (j, 0, g)),
                pl.BlockSpec((None, M_CONV, M_STATE), lambda s, g: (j, 0, gx + g)),
                pl.BlockSpec((None, M_CONV, M_STATE), lambda s, g: (j, 0, gc + g)),
                pl.BlockSpec((None, 1, M_GROUP_CH), lambda s, g: (j, 0, g)),
                pl.BlockSpec((None, 1, M_STATE), lambda s, g: (j, 0, gx + g)),
                pl.BlockSpec((None, 1, M_STATE), lambda s, g: (j, 0, gc + g)),
                pl.BlockSpec((None, 1, LANES), lambda s, g: (j, 0, g)),
                pl.BlockSpec((None, 1, LANES), lambda s, g: (j, 0, g)),
                pl.BlockSpec((None, 1, M_GROUP_CH), lambda s, g: (j, 0, g)),
                pl.BlockSpec((None, 1, M_GROUP_CH), lambda s, g: (j, 0, g))]
    args = [xbc, xbc, xbc, z, dt, conv_w, conv_w, conv_w, conv_b, conv_b, conv_b,
            dtb, alog, dsk, norm_g]
    if has_h0:
        in_specs.append(pl.BlockSpec(st_block, lambda s, g: (s, 0, g, 0)))
        args.append(h0)
    out_specs = [pl.BlockSpec((seq_len, M_GROUP_CH), lambda s, g: (s, g))]
    out_shape = [jax.ShapeDtypeStruct((rows, M_INNER), BF16)]
    if emit_state:
        out_specs.append(pl.BlockSpec(st_block, lambda s, g: (s, 0, g, 0)))
        out_shape.append(jax.ShapeDtypeStruct((n_seq, 2, M_INNER, M_STATE), F32))
    scratch = [pltpu.VMEM((seq_len, M_GROUP_CH), F32),
               pltpu.VMEM((n_chunks, M_GROUP_CH, M_CHUNK), F32),
               pltpu.VMEM((seq_len, M_STATE), BF16),
               pltpu.VMEM((seq_len, M_STATE), BF16),
               pltpu.VMEM((2, seq_len, LANES), F32),
               pltpu.VMEM((2, n_chunks, M_GROUP_HEADS, M_CHUNK), F32),
               pltpu.VMEM((2, n_chunks, M_GROUP_HEADS, M_CHUNK), F32),
               pltpu.VMEM((n_chunks, M_GROUP_CH, M_CHUNK), F32),
               pltpu.VMEM((2, M_GROUP_CH, M_STATE), F32)]
    return pl.pallas_call(
        functools.partial(_ssd_kernel, seq_len=seq_len, has_h0=has_h0, emit_state=emit_state),
        grid=(n_seq, M_GROUPS),
        in_specs=in_specs,
        out_specs=out_specs,
        out_shape=out_shape,
        scratch_shapes=scratch,
        compiler_params=_cparams(2),
        name="ssd",
    )(*args)


def _gmlp_kernel(x_ref, mod_ref, g_ref, win_ref, bin_ref, ng_ref, ws_ref, bs_ref, wout_ref,
                 o_ref, gate_s, *, tm):
    x = x_ref[...]
    h = _modnorm(x, g_ref[1:2, :], mod_ref[3:4, :], mod_ref[4:5, :]).astype(BF16)
    hg = jax.nn.gelu(_dot(h, win_ref[...]) + bin_ref[...])
    u = hg[:, :G_INNER]
    v = hg[:, G_INNER:]
    v = v * lax.rsqrt(jnp.mean(v * v, axis=-1, keepdims=True) + EPS) * ng_ref[...]
    vb = v.astype(BF16)
    bs = bs_ref[...]
    for c in range(tm // G_CHUNK):
        r = slice(c * G_CHUNK, (c + 1) * G_CHUNK)
        for hd in range(G_HEADS):
            cs = slice(hd * G_HEAD_CH, (hd + 1) * G_HEAD_CH)
            sv = _dot(ws_ref[hd], vb[r, cs]) + bs[:, hd:hd + 1]
            gate_s[r, cs] = (u[r, cs] * sv).astype(BF16)
    o_ref[...] = x + mod_ref[5:6, :] * _dot(gate_s[...], wout_ref[...])


def _gmlp(x, mods, ln_g, w_in, b_in, norm_g, w_s, b_s_t, w_out, layer, j, pop):
    tm = 512
    rows = x.shape[0]
    return pl.pallas_call(
        functools.partial(_gmlp_kernel, tm=tm),
        grid=(rows // tm,),
        in_specs=[pl.BlockSpec((tm, D_MODEL), lambda t: (t, 0)),
                  _mod_spec(layer, pop, tm),
                  _ln_spec(layer),
                  _resident((D_MODEL, 2 * G_INNER), lambda t: (0, 0)),
                  _resident((None, 1, 2 * G_INNER), lambda t: (j, 0, 0)),
                  _resident((None, 1, G_INNER), lambda t: (j, 0, 0)),
                  _resident((None, G_HEADS, G_CHUNK, G_CHUNK), lambda t: (j, 0, 0, 0)),
                  _resident((None, G_CHUNK, G_HEADS), lambda t: (j, 0, 0)),
                  _resident((G_INNER, D_MODEL), lambda t: (0, 0))],
        out_specs=pl.BlockSpec((tm, D_MODEL), lambda t: (t, 0)),
        out_shape=jax.ShapeDtypeStruct((rows, D_MODEL), F32),
        scratch_shapes=[pltpu.VMEM((tm, G_INNER), BF16)],
        compiler_params=_cparams(1),
        name="gmlp",
    )(x, mods, ln_g, w_in, b_in, norm_g, w_s, b_s_t, w_out)


def _dft_tables(seq_len):
    def angles(n):
        k = np.arange(n, dtype=np.int64)
        return 2.0 * np.pi * ((k[:, None] * k[None, :]) % n).astype(np.float64) / n
    ac = angles(F_GROUP_CH)
    al = angles(seq_len)
    chan = np.concatenate([np.cos(ac), np.sin(ac)], axis=1).astype(np.float32)
    pos = np.concatenate([np.cos(al), -np.sin(al)], axis=1).astype(np.float32)
    return chan, pos


def _fnet_kernel(x_ref, mod_ref, g_ref, chan_ref, pos_ref, wout_ref, b_ref, o_ref, f_s,
                 *, seq_len):
    x = x_ref[...]
    h = _modnorm(x, g_ref[1:2, :], mod_ref[3:4, :], mod_ref[4:5, :]).astype(BF16)
    chan = chan_ref[...].astype(BF16)
    pos = pos_ref[...].astype(BF16)
    scale = 1.0 / math.sqrt(seq_len * F_GROUP_CH)
    for g in range(F_GROUPS):
        cs = slice(g * F_GROUP_CH, (g + 1) * F_GROUP_CH)
        p = _dot(h[:, cs], chan)
        stacked = jnp.concatenate([p[:, :F_GROUP_CH], p[:, F_GROUP_CH:]], axis=0).astype(BF16)
        f_s[:, cs] = (_dot(pos, stacked) * scale).astype(BF16)
    o_ref[...] = x + mod_ref[5:6, :] * (_dot(f_s[...], wout_ref[...]) + b_ref[...])


def _fnet(x, mods, ln_g, w_out, b_out, layer, j, pop, seq_len):
    rows = x.shape[0]
    chan, pos = _dft_tables(seq_len)
    return pl.pallas_call(
        functools.partial(_fnet_kernel, seq_len=seq_len),
        grid=(rows // seq_len,),
        in_specs=[pl.BlockSpec((seq_len, D_MODEL), lambda t: (t, 0)),
                  _mod_spec(layer, pop, seq_len),
                  _ln_spec(layer),
                  _resident(chan.shape, lambda t: (0, 0)),
                  _resident(pos.shape, lambda t: (0, 0)),
                  _resident((D_MODEL, D_MODEL), lambda t: (0, 0)),
                  _resident((None, 1, D_MODEL), lambda t: (j, 0, 0))],
        out_specs=pl.BlockSpec((seq_len, D_MODEL), lambda t: (t, 0)),
        out_shape=jax.ShapeDtypeStruct((rows, D_MODEL), F32),
        scratch_shapes=[pltpu.VMEM((seq_len, D_MODEL), BF16)],
        compiler_params=_cparams(1),
        name="fnet",
    )(x, mods, ln_g, jnp.asarray(chan), jnp.asarray(pos), w_out, b_out)


def _head_mean_matrix():
    i = np.arange(2 * LANES)
    return ((i[:, None] // A_HD) == (i[None, :] // A_HD)).astype(np.float32) / A_HD


def _rope_tables():
    half = A_HD // 4
    inv = (np.float32(ROPE_THETA) ** (-np.arange(half, dtype=np.float32) / np.float32(half)))
    inv = inv.astype(np.float32)
    t = np.arange(DEC_SEQ)
    lane = np.arange(LANES)
    hl = lane % A_HD
    posn = np.where(hl[None, :] < A_HD // 2, (t // GRID_W)[:, None], (t % GRID_W)[:, None])
    sub = hl % (A_HD // 2)
    ang = posn.astype(np.float32) * inv[sub % half][None, :]
    cos = np.cos(ang).astype(np.float32)
    sin = np.sin(ang).astype(np.float32)
    lower = (sub < half)[None, :]
    sin_up = np.where(lower, -sin, 0.0).astype(np.float32)
    sin_dn = np.where(lower, 0.0, sin).astype(np.float32)
    return cos, sin_up, sin_dn


def _qkv_kernel(*refs, rope):
    it = iter(refs)
    x_ref, mod_ref, g_ref, w_ref, qg_ref, kg_ref, avg_ref = (next(it) for _ in range(7))
    if rope:
        cos_ref, sup_ref, sdn_ref = (next(it) for _ in range(3))
    q_ref, k_ref, v_ref = (next(it) for _ in range(3))
    h = _modnorm(x_ref[...], g_ref[1:2, :], mod_ref[3:4, :], mod_ref[4:5, :]).astype(BF16)
    qkv = _dot(h, w_ref[...])
    avg = avg_ref[...]
    half = A_HD // 4

    def head_norm(t, gain):
        hi, lo = _split2(t * t)
        ms = _dot(hi, avg) + _dot(lo, avg)
        return t * lax.rsqrt(ms + EPS) * gain

    def rotate(t):
        up = pltpu.roll(t, LANES - half, 1)
        dn = pltpu.roll(t, half, 1)
        return t * cos_ref[...] + up * sup_ref[...] + dn * sdn_ref[...]

    slab = 2 * LANES
    for s in range((A_Q + A_KVW) // slab):
        cs = slice(s * slab, (s + 1) * slab)
        is_q = s < A_Q // slab
        gain = qg_ref[:, cs] if is_q else kg_ref[...]
        t = head_norm(qkv[:, cs], gain)
        if rope:
            t = jnp.concatenate([rotate(t[:, :LANES]), rotate(t[:, LANES:])], axis=1)
        if is_q:
            q_ref[:, cs] = t.astype(q_ref.dtype)
        else:
            k_ref[...] = t.astype(k_ref.dtype)
    v_ref[...] = qkv[:, A_Q + A_KVW:].astype(v_ref.dtype)


def _qkv(x, mods, ln_g, w, q_gain, k_gain, layer, j, pop):
    tm = 512
    rows = x.shape[0]
    rope = pop == 1
    kv_dtype = BF16 if rope else F32
    in_specs = [pl.BlockSpec((tm, D_MODEL), lambda t: (t, 0)),
                _mod_spec(layer, pop, tm),
                _ln_spec(layer),
                _resident((D_MODEL, A_Q + 2 * A_KVW), lambda t: (0, 0)),
                _resident((None, 1, A_Q), lambda t: (j, 0, 0)),
                _resident((None, 1, A_KVW), lambda t: (j, 0, 0)),
                _resident((2 * LANES, 2 * LANES), lambda t: (0, 0))]
    args = [x, mods, ln_g, w, q_gain, k_gain, jnp.asarray(_head_mean_matrix(), BF16)]
    if rope:
        per_seq = DEC_SEQ // tm
        in_specs += [pl.BlockSpec((tm, LANES), lambda t: (t % per_seq, 0))] * 3
        args += [jnp.asarray(a) for a in _rope_tables()]
    return pl.pallas_call(
        functools.partial(_qkv_kernel, rope=rope),
        grid=(rows // tm,),
        in_specs=in_specs,
        out_specs=[pl.BlockSpec((tm, A_Q), lambda t: (t, 0)),
                   pl.BlockSpec((tm, A_KVW), lambda t: (t, 0)),
                   pl.BlockSpec((tm, A_KVW), lambda t: (t, 0))],
        out_shape=[jax.ShapeDtypeStruct((rows, A_Q), BF16),
                   jax.ShapeDtypeStruct((rows, A_KVW), kv_dtype),
                   jax.ShapeDtypeStruct((rows, A_KVW), kv_dtype)],
        compiler_params=_cparams(1),
        name="qkv",
    )(*args)


def _head_place_matrices():
    m = np.zeros((A_KV, A_KVW, A_GRP * A_HD), np.float32)
    d = np.arange(A_HD)
    for j in range(A_KV):
        for g in range(A_GRP):
            m[j, j * A_HD + d, g * A_HD + d] = 1.0
    return m


def _attn_kernel(q_ref, place_ref, *refs, n_src):
    kv_refs, o_ref = refs[:2 * n_src], refs[2 * n_src]
    lane = lax.broadcasted_iota(jnp.int32, (1, A_GRP * A_HD), 1)
    scale = A_HD ** -0.5
    for j in range(A_KV):
        place = place_ref[j]
        qs = q_ref[:, j * A_GRP * A_HD:(j + 1) * A_GRP * A_HD]
        k4 = [_dot(kv_refs[2 * s][...].astype(BF16), place).astype(BF16) for s in range(n_src)]
        v4 = [_dot(kv_refs[2 * s + 1][...].astype(BF16), place).astype(BF16) for s in range(n_src)]
        acc = None
        for g in range(A_GRP):
            own = (lane >= g * A_HD) & (lane < (g + 1) * A_HD)
            qm = jnp.where(own, qs, jnp.zeros_like(qs))
            sc = [_dot_nt(qm, k4[s]) * scale for s in range(n_src)]
            mx = functools.reduce(jnp.maximum, [jnp.max(t, axis=-1, keepdims=True) for t in sc])
            p = [jnp.exp(t - mx) for t in sc]
            den = functools.reduce(jnp.add, [jnp.sum(t, axis=-1, keepdims=True) for t in p])
            pv = functools.reduce(jnp.add, [
                _dot(p[s].astype(BF16), jnp.where(own, v4[s], jnp.zeros_like(v4[s])))
                for s in range(n_src)])
            acc = pv / den if acc is None else acc + pv / den
        o_ref[:, j * A_GRP * A_HD:(j + 1) * A_GRP * A_HD] = acc.astype(o_ref.dtype)


def _attn(q, sources, seq_len, tq):
    rows = q.shape[0]
    n_batch = rows // seq_len
    per = seq_len // tq
    in_specs = [pl.BlockSpec((tq, A_Q), lambda b, t: (b * per + t, 0)),
                _resident((A_KV, A_KVW, A_GRP * A_HD), lambda b, t: (0, 0, 0))]
    args = [q, jnp.asarray(_head_place_matrices(), BF16)]
    for k, v, lk in sources:
        in_specs += [pl.BlockSpec((lk, A_KVW), lambda b, t: (b, 0))] * 2
        args += [k, v]
    return pl.pallas_call(
        functools.partial(_attn_kernel, n_src=len(sources)),
        grid=(n_batch, per),
        in_specs=in_specs,
        out_specs=pl.BlockSpec((tq, A_Q), lambda b, t: (b * per + t, 0)),
        out_shape=jax.ShapeDtypeStruct((rows, A_Q), BF16),
        compiler_params=_cparams(2),
        name="attn",
    )(*args)


def _regroup_heads(a):
    lead = a.shape[:-1]
    a = a.reshape(lead + (2, M_GROUPS, M_GROUP_HEADS))
    a = jnp.swapaxes(a, -3, -2).reshape(lead + (M_GROUPS, 2 * M_GROUP_HEADS))
    pad = [(0, 0)] * (len(lead) + 1) + [(0, LANES - 2 * M_GROUP_HEADS)]
    return jnp.pad(a, pad).reshape(lead + (M_GROUPS * LANES,))


def kernel(x_prompt, x_sample, state_ssm, cache_k, cache_v, c, c_ctx,
           ln_g, ada_w, ada_b, ff1_w_in, ff1_w_out, ff2_w_in, ff2_w_out,
           m_w_in, m_conv_w, m_conv_b, m_dt_bias, m_a_log, m_d, m_norm_g, m_w_out,
           g_w_in, g_b_in, g_norm_g, g_w_s, g_b_s, g_w_out,
           f_w_out, f_b_out,
           a_w_qkv, a_q_norm, a_k_norm, a_w_o):
    xs = [x_prompt.reshape(TP, D_MODEL), x_sample.reshape(TS, D_MODEL)]
    seq = (SEQ, DEC_SEQ)

    conds = jnp.concatenate(
        [c_ctx[None], c, jnp.zeros((COND_ROWS - 1 - DEC_BATCH, D_MODEL), F32)], axis=0)
    mods = _adaln(conds, ada_w, ada_b)

    ff_w = {(0, 0): (ff1_w_in[0].astype(BF16), ff1_w_out[0].astype(BF16))}
    n_zx = M_INNER + M_CONV_CH
    mixer_weights = ([(m_w_in, n_zx), (m_w_out, D_MODEL)],
                     [(g_w_in, 2 * G_INNER), (g_w_out, D_MODEL)],
                     [(f_w_out, D_MODEL)],
                     [(a_w_qkv, A_Q + 2 * A_KVW), (a_w_o, D_MODEL)])

    new_ssm, new_k, new_v = [], [], []
    for i in range(DEPTH):
        kind, j = i % N_MIXERS, i // N_MIXERS
        later = [(i, 1)] + ([(i + 1, 0)] if i + 1 < DEPTH else [])
        convert = []
        for layer, second in later:
            pair = (ff2_w_in, ff2_w_out) if second else (ff1_w_in, ff1_w_out)
            convert += [(pair[0], layer, 2 * FFN_HIDDEN), (pair[1], layer, D_MODEL)]
        convert += [(w, j, n_cols) for w, n_cols in mixer_weights[kind]]
        w_in, w_out = ff_w[(i, 0)]
        xs[0], cast = _ffn(xs[0], mods, ln_g, w_in, w_out, i, False, 0, convert=convert)
        for n, key in enumerate(later):
            ff_w[key] = (cast[2 * n], cast[2 * n + 1])
        mixer_w = cast[2 * len(later):]
        xs[1], _ = _ffn(xs[1], mods, ln_g, w_in, w_out, i, False, 1)
        mix = [None, None]
        if kind == 0:
            w_zx, w_out = mixer_w
            w_dt = _regroup_heads(m_w_in[j, :, n_zx:]).astype(BF16)
            dtb = _regroup_heads(m_dt_bias.reshape(-1, 1, 2 * M_HEADS))
            alog = _regroup_heads(m_a_log.reshape(-1, 1, 2 * M_HEADS))
            dsk = jnp.repeat(m_d, M_HEADDIM, axis=-1)[:, None, :]
            conv_b = m_conv_b[:, None, :]
            norm_g = m_norm_g[:, None, :]
            h0 = state_ssm[:, j].reshape(DEC_BATCH, 2, M_INNER, M_STATE)
            for p in range(2):
                z, xbc, dt = _modlinear(xs[p], mods, ln_g, [w_zx, w_dt], i, p,
                                        ((M_INNER, M_CONV_CH), (M_GROUPS * LANES,)))
                res = _ssd(xbc, z, dt, m_conv_w, conv_b, dtb, alog, dsk, norm_g,
                           h0 if p == 1 else None, j, seq[p], emit_state=(p == 0))
                if p == 0:
                    new_ssm.append(res[1].reshape(BATCH, 2, M_HEADS, M_HEADDIM, M_STATE))
                mix[p] = (res[0], w_out)
        elif kind == 1:
            w_in, w_out = mixer_w
            w_s = g_w_s.astype(BF16)
            b_s_t = jnp.swapaxes(g_b_s, -1, -2)
            xs = [_gmlp(xs[p], mods, ln_g, w_in, g_b_in[:, None, :], g_norm_g[:, None, :],
                        w_s, b_s_t, w_out, i, j, p) for p in range(2)]
        elif kind == 2:
            w_out, = mixer_w
            xs = [_fnet(xs[p], mods, ln_g, w_out, f_b_out[:, None, :], i, j, p, seq[p])
                  for p in range(2)]
        else:
            w_qkv, w_o = mixer_w
            q_gain = jnp.tile(a_q_norm, (1, A_HEADS))[:, None, :]
            k_gain = jnp.tile(a_k_norm, (1, A_KV))[:, None, :]
            qp, kp, vp = _qkv(xs[0], mods, ln_g, w_qkv, q_gain, k_gain, i, j, 0)
            new_k.append(kp.reshape(BATCH, SEQ, A_KV, A_HD))
            new_v.append(vp.reshape(BATCH, SEQ, A_KV, A_HD))
            op = _attn(qp, [(kp, vp, SEQ)], SEQ, SEQ)
            ql, kl, vl = _qkv(xs[1], mods, ln_g, w_qkv, q_gain, k_gain, i, j, 1)
            kc = cache_k[:, j].reshape(DEC_BATCH * PAST_LEN, A_KVW)
            vc = cache_v[:, j].reshape(DEC_BATCH * PAST_LEN, A_KVW)
            ol = _attn(ql, [(kc, vc, PAST_LEN), (kl, vl, DEC_SEQ)], DEC_SEQ, 256)
            mix = [(op, w_o), (ol, w_o)]
        w_in, w_out = ff_w[(i, 1)]
        xs = [_ffn(xs[p], mods, ln_g, w_in, w_out, i, True, p, mix[p])[0] for p in range(2)]

    return (xs[0].reshape(BATCH, SEQ, D_MODEL),
            xs[1].reshape(DEC_BATCH, DEC_SEQ, D_MODEL),
            jnp.stack(new_ssm, axis=1),
            jnp.stack(new_k, axis=1),
            jnp.stack(new_v, axis=1))
```

```python
import functools
import math

import numpy as np
import jax
import jax.numpy as jnp
from jax import lax
from jax.experimental import pallas as pl
from jax.experimental.pallas import tpu as pltpu

F32 = jnp.float32
BF16 = jnp.bfloat16

D_MODEL = 1024
BATCH = 32
SEQ = 256
DEPTH = 4
DEC_BATCH = 2
DEC_SEQ = 1024
PAST_LEN = 256
GRID_W = 64
N_MIXERS = 4
N_MOD = 9
EPS = 1e-6
FFN_HIDDEN = 2816

M_INNER = 2 * D_MODEL
M_HEADDIM = 64
M_HEADS = M_INNER // M_HEADDIM
M_GROUPS = 4
M_STATE = 128
M_CONV = 3
M_CHUNK = 128
M_CONV_CH = M_INNER + 2 * M_GROUPS * M_STATE
M_GROUP_CH = M_INNER // M_GROUPS
M_GROUP_HEADS = M_HEADS // M_GROUPS

G_CHUNK = 128
G_INNER = 2 * D_MODEL
G_HEADS = 8
G_HEAD_CH = G_INNER // G_HEADS

F_GROUPS = 4
F_GROUP_CH = D_MODEL // F_GROUPS

A_HEADS = 16
A_KV = 4
A_HD = 64
A_GRP = A_HEADS // A_KV
A_Q = A_HEADS * A_HD
A_KVW = A_KV * A_HD
ROPE_THETA = 10000.0

LANES = 128
COND_ROWS = 8
TP = BATCH * SEQ
TS = DEC_BATCH * DEC_SEQ
VMEM_LIMIT = 56 * 2**20


def _cparams(n_grid):
    return pltpu.CompilerParams(dimension_semantics=("arbitrary",) * n_grid,
                                vmem_limit_bytes=VMEM_LIMIT)


def _resident(block_shape, index_map):
    return pl.BlockSpec(block_shape, index_map, pipeline_mode=pl.Buffered(1))


def _dot(a, b):
    return jnp.dot(a, b, preferred_element_type=F32)


def _dot_nt(a, b):
    return lax.dot_general(a, b, (((1,), (1,)), ((), ())), preferred_element_type=F32)


def _silu(x):
    return x * jax.nn.sigmoid(x)


def _softplus(x):
    return jnp.maximum(x, 0.0) + jnp.log1p(jnp.exp(-jnp.abs(x)))


def _modnorm(x, g, shift, scale):
    y = x * lax.rsqrt(jnp.mean(x * x, axis=-1, keepdims=True) + EPS) * g
    return y * (1.0 + scale) + shift


def _split3(v):
    hi = v.astype(BF16)
    r1 = v - hi.astype(F32)
    mid = r1.astype(BF16)
    lo = (r1 - mid.astype(F32)).astype(BF16)
    return hi, mid, lo


def _split2(v):
    hi = v.astype(BF16)
    return hi, (v - hi.astype(F32)).astype(BF16)


def _adaln_kernel(c_ref, w_ref, b_ref, o_ref):
    s = _silu(c_ref[...]).astype(BF16)
    o_ref[...] = _dot(s, w_ref[...].astype(BF16)) + b_ref[...]


def _adaln(conds, ada_w, ada_b):
    tn = 2304
    n = N_MOD * D_MODEL
    out = pl.pallas_call(
        _adaln_kernel,
        grid=(DEPTH, n // tn),
        in_specs=[pl.BlockSpec((COND_ROWS, D_MODEL), lambda i, j: (0, 0)),
                  pl.BlockSpec((None, D_MODEL, tn), lambda i, j: (i, 0, j)),
                  pl.BlockSpec((None, 1, tn), lambda i, j: (i, 0, j))],
        out_specs=pl.BlockSpec((None, COND_ROWS, tn), lambda i, j: (i, 0, j)),
        out_shape=jax.ShapeDtypeStruct((DEPTH, COND_ROWS, n), F32),
        compiler_params=_cparams(2),
        name="adaln",
    )(conds, ada_w, ada_b.reshape(DEPTH, 1, n))
    return out.reshape(DEPTH, COND_ROWS, N_MOD, D_MODEL)


def _mod_spec(layer, pop, tm):
    if pop == 0:
        return pl.BlockSpec((None, None, N_MOD, D_MODEL), lambda t, *_: (layer, 0, 0, 0))
    return pl.BlockSpec((None, None, N_MOD, D_MODEL),
                        lambda t, *_: (layer, 1 + (t * tm) // DEC_SEQ, 0, 0))


def _ln_spec(layer):
    return pl.BlockSpec((None, 3, D_MODEL), lambda t, *_: (layer, 0, 0))


MXU_COLS = 256
FFN_PIECES = tuple((lo, min(lo + 3 * MXU_COLS, FFN_HIDDEN))
                   for lo in range(0, FFN_HIDDEN, 3 * MXU_COLS))


def _ffn_kernel(*refs, k0, gk, has_mix, n_convert):
    it = iter(refs)
    if has_mix:
        y_ref, wmix_ref = next(it), next(it)
    x_ref, mod_ref, g_ref, win_hbm, wout_hbm = (next(it) for _ in range(5))
    src_refs = [next(it) for _ in range(n_convert)]
    o_ref = next(it)
    dst_refs = [next(it) for _ in range(n_convert)]
    win_ref, wout_ref, sem = (next(it) for _ in range(3))
    first = pl.program_id(0) == 0

    def piece_copies(p):
        lo, hi = FFN_PIECES[p]
        cols = pl.ds(lo, hi - lo)
        up_cols = pl.ds(FFN_HIDDEN + lo, hi - lo)
        return (pltpu.make_async_copy(win_hbm.at[:, cols], win_ref.at[:, cols], sem.at[p, 0]),
                pltpu.make_async_copy(win_hbm.at[:, up_cols], win_ref.at[:, up_cols], sem.at[p, 1]),
                pltpu.make_async_copy(wout_hbm.at[cols, :], wout_ref.at[cols, :], sem.at[p, 2]))

    @pl.when(first)
    def _():
        for p in range(len(FFN_PIECES)):
            for cp in piece_copies(p):
                cp.start()

    x = x_ref[...]
    if has_mix:
        x = x + mod_ref[5:6, :] * _dot(y_ref[...], wmix_ref[...])
    h = _modnorm(x, g_ref[gk:gk + 1, :], mod_ref[k0:k0 + 1, :], mod_ref[k0 + 1:k0 + 2, :])
    h = h.astype(BF16)

    def half_step(wait_for_weights):
        o = None
        for p, (lo, hi) in enumerate(FFN_PIECES):
            if wait_for_weights:
                for cp in piece_copies(p):
                    cp.wait()
            gate = _dot(h, win_ref[:, lo:hi])
            up = _dot(h, win_ref[:, FFN_HIDDEN + lo:FFN_HIDDEN + hi])
            part = _dot((_silu(gate) * up).astype(BF16), wout_ref[lo:hi, :])
            o = part if o is None else o + part
        o_ref[...] = x + 0.5 * mod_ref[k0 + 2:k0 + 3, :] * o

    pl.when(first)(lambda: half_step(True))
    pl.when(jnp.logical_not(first))(lambda: half_step(False))
    for src, dst in zip(src_refs, dst_refs):
        dst[...] = src[...].astype(BF16)


def _ffn(x, mods, ln_g, w_in, w_out, layer, second, pop, mix=None, convert=()):
    tm = 512
    rows = x.shape[0]
    steps = rows // tm
    in_specs = [pl.BlockSpec((tm, D_MODEL), lambda t: (t, 0)),
                _mod_spec(layer, pop, tm),
                _ln_spec(layer),
                pl.BlockSpec(memory_space=pl.ANY),
                pl.BlockSpec(memory_space=pl.ANY)]
    args = [x, mods, ln_g, w_in, w_out]
    if mix is not None:
        y, w_mix = mix
        k = y.shape[1]
        in_specs = [pl.BlockSpec((tm, k), lambda t: (t, 0)),
                    _resident((k, D_MODEL), lambda t: (0, 0))] + in_specs
        args = [y, w_mix] + args
    out_specs = [pl.BlockSpec((tm, D_MODEL), lambda t: (t, 0))]
    out_shape = [jax.ShapeDtypeStruct((rows, D_MODEL), F32)]
    for src, index, n_cols in convert:
        n_rows = src.shape[1]
        slab = n_rows // steps
        in_specs.append(pl.BlockSpec((None, slab, n_cols), lambda t, index=index: (index, t, 0)))
        args.append(src)
        out_specs.append(pl.BlockSpec((slab, n_cols), lambda t: (t, 0)))
        out_shape.append(jax.ShapeDtypeStruct((n_rows, n_cols), BF16))
    res = pl.pallas_call(
        functools.partial(_ffn_kernel, k0=6 if second else 0, gk=2 if second else 0,
                          has_mix=mix is not None, n_convert=len(convert)),
        grid=(steps,),
        in_specs=in_specs,
        out_specs=out_specs,
        out_shape=out_shape,
        scratch_shapes=[pltpu.VMEM((D_MODEL, 2 * FFN_HIDDEN), BF16),
                        pltpu.VMEM((FFN_HIDDEN, D_MODEL), BF16),
                        pltpu.SemaphoreType.DMA((len(FFN_PIECES), 3))],
        compiler_params=_cparams(1),
        name="ffn",
    )(*args)
    return res[0], list(res[1:])


def _modlinear_kernel(x_ref, mod_ref, g_ref, *refs, widths):
    w_refs, o_refs = refs[:len(widths)], iter(refs[len(widths):])
    h = _modnorm(x_ref[...], g_ref[1:2, :], mod_ref[3:4, :], mod_ref[4:5, :]).astype(BF16)
    for w_ref, group in zip(w_refs, widths):
        off = 0
        for wd in group:
            next(o_refs)[...] = _dot(h, w_ref[:, off:off + wd])
            off += wd


def _modlinear(x, mods, ln_g, weights, layer, pop, widths):
    tm = 512
    rows = x.shape[0]
    flat = [wd for group in widths for wd in group]
    return pl.pallas_call(
        functools.partial(_modlinear_kernel, widths=widths),
        grid=(rows // tm,),
        in_specs=[pl.BlockSpec((tm, D_MODEL), lambda t: (t, 0)),
                  _mod_spec(layer, pop, tm),
                  _ln_spec(layer)]
                 + [_resident(w.shape, lambda t: (0, 0)) for w in weights],
        out_specs=[pl.BlockSpec((tm, wd), lambda t: (t, 0)) for wd in flat],
        out_shape=[jax.ShapeDtypeStruct((rows, wd), F32) for wd in flat],
        compiler_params=_cparams(1),
        name="modlinear",
    )(x, mods, ln_g, *weights)


def _ssd_kernel(*refs, seq_len, has_h0, emit_state):
    it = iter(refs)
    x_ref, b_ref, c_ref, z_ref, dt_ref = (next(it) for _ in range(5))
    cwx_ref, cwb_ref, cwc_ref, cbx_ref, cbb_ref, cbc_ref = (next(it) for _ in range(6))
    dtb_ref, alog_ref, dsk_ref, ng_ref = (next(it) for _ in range(4))
    h0_ref = next(it) if has_h0 else None
    y_ref = next(it)
    st_ref = next(it) if emit_state else None
    xs_s, xt_s, b_s, c_s, acum_s, a8_s, dt8_s, yt_s, h_s = (next(it) for _ in range(9))
    n_chunks = seq_len // M_CHUNK

    def conv_silu(ref, w_ref, bias_ref):
        v = ref[...]
        t = lax.broadcasted_iota(jnp.int32, v.shape, 0)
        prev = jnp.where(t == 0, 0.0, pltpu.roll(v, 1, 0))
        nxt = jnp.where(t == seq_len - 1, 0.0, pltpu.roll(v, seq_len - 1, 0))
        w = w_ref[...]
        return _silu(prev * w[0:1, :] + v * w[1:2, :] + nxt * w[2:3, :] + bias_ref[...])

    xs = conv_silu(x_ref, cwx_ref, cbx_ref)
    xs_s[...] = xs
    b_s[...] = conv_silu(b_ref, cwb_ref, cbb_ref).astype(BF16)
    c_s[...] = conv_silu(c_ref, cwc_ref, cbc_ref).astype(BF16)
    dt = _softplus(dt_ref[...] + dtb_ref[...])
    dta = dt * (-jnp.exp(alog_ref[...]))
    row = lax.broadcasted_iota(jnp.int32, (M_CHUNK, M_CHUNK), 0)
    col = lax.broadcasted_iota(jnp.int32, (M_CHUNK, M_CHUNK), 1)
    keeps = (col <= row, col >= row)
    tris = [jnp.where(k, 1.0, 0.0).astype(BF16) for k in keeps]
    for c in range(n_chunks):
        r = slice(c * M_CHUNK, (c + 1) * M_CHUNK)
        dt_t = dt[r, :].T
        by_time = _split3(dta[r, :])
        by_head = _split3(dta[r, :].T)
        for d in range(2):
            hl = slice(M_GROUP_HEADS * d, M_GROUP_HEADS * (d + 1))
            acum_s[d, r, :] = functools.reduce(jnp.add, [_dot(tris[d], p) for p in by_time])
            acum_t = functools.reduce(jnp.add, [_dot_nt(p, tris[d]) for p in by_head])
            a8_s[d, c] = acum_t[hl, :]
            dt8_s[d, c] = dt_t[hl, :]
        for k in range(M_GROUP_CH // LANES):
            xt_s[c, k * LANES:(k + 1) * LANES, :] = xs[r, k * LANES:(k + 1) * LANES].T
    yt_s[...] = jnp.zeros(yt_s.shape, F32)
    if has_h0:
        h_s[...] = h0_ref[...]
    else:
        h_s[...] = jnp.zeros(h_s.shape, F32)

    def head_rows(v8):
        return jnp.concatenate(
            [jnp.broadcast_to(v8[r:r + 1, :], (M_HEADDIM, v8.shape[1]))
             for r in range(M_GROUP_HEADS)], axis=0)

    def chunk_step(c, d):
        start = c * M_CHUNK
        rows = pl.ds(start if isinstance(c, int) else pl.multiple_of(start, M_CHUNK), M_CHUNK)
        keep = keeps[d]
        end = M_CHUNK - 1 if d == 0 else 0
        acum = acum_s[d, rows, :]
        a8 = a8_s[d, c]
        dt8 = dt8_s[d, c]
        a_end8 = jnp.broadcast_to(a8[:, end:end + 1], a8.shape)
        w8 = jnp.exp(a_end8 - a8) * dt8
        bc = b_s[rows, :]
        cc = c_s[rows, :]
        cb = _dot_nt(cc, bc)
        h_in = h_s[d]
        h_s[d] = (h_in * head_rows(jnp.exp(a_end8))
                  + _dot((xt_s[c] * head_rows(w8)).astype(BF16), bc))
        yt_s[c] += _dot_nt(h_in.astype(BF16), cc) * head_rows(jnp.exp(a8))
        for r in range(M_GROUP_HEADS):
            ln = M_GROUP_HEADS * d + r
            hs = slice(r * M_HEADDIM, (r + 1) * M_HEADDIM)
            seg = jnp.broadcast_to(acum[:, ln:ln + 1], (M_CHUNK, M_CHUNK)) - a8[r:r + 1, :]
            decay = jnp.exp(jnp.where(keep, seg, -jnp.inf))
            m = (cb * decay * dt8[r:r + 1, :]).astype(BF16)
            yt_s[c, hs, :] += _dot_nt(xt_s[c, hs, :].astype(BF16), m)

    def both_directions(i):
        chunk_step(i, 0)
        chunk_step(n_chunks - 1 - i, 1)

    if n_chunks <= 2:
        for i in range(n_chunks):
            both_directions(i)
    else:
        def body(i, carry):
            both_directions(i)
            return carry
        lax.fori_loop(0, n_chunks, body, 0)

    for c in range(n_chunks):
        r = slice(c * M_CHUNK, (c + 1) * M_CHUNK)
        for k in range(M_GROUP_CH // LANES):
            cs = slice(k * LANES, (k + 1) * LANES)
            xs_s[r, cs] = xs_s[r, cs] * dsk_ref[:, cs] + yt_s[c, cs, :].T
    y = xs_s[...] * _silu(z_ref[...])
    y = y * lax.rsqrt(jnp.mean(y * y, axis=-1, keepdims=True) + EPS) * ng_ref[...]
    y_ref[...] = y.astype(BF16)
    if emit_state:
        st_ref[...] = h_s[...]


def _ssd(xbc, z, dt, conv_w, conv_b, dtb, alog, dsk, norm_g, h0, j, seq_len, emit_state):
    rows = xbc.shape[0]
    n_seq = rows // seq_len
    gx = M_INNER // LANES
    gc = gx + M_GROUPS * M_STATE // LANES
    has_h0 = h0 is not None
    n_chunks = seq_len // M_CHUNK
    st_block = (None, 2, M_GROUP_CH, M_STATE)

    in_specs = [pl.BlockSpec((seq_len, M_GROUP_CH), lambda s, g: (s, g)),
                pl.BlockSpec((seq_len, M_STATE), lambda s, g: (s, gx + g)),
                pl.BlockSpec((seq_len, M_STATE), lambda s, g: (s, gc + g)),
                pl.BlockSpec((seq_len, M_GROUP_CH), lambda s, g: (s, g)),
                pl.BlockSpec((seq_len, LANES), lambda s, g: (s, g)),
                pl.BlockSpec((None, M_CONV, M_GROUP_CH), lambda s, g: (j, 0, g)),
                pl.BlockSpec((None, M_CONV, M_STATE), lambda s, g: (j, 0, gx + g)),
                pl.BlockSpec((None, M_CONV, M_STATE), lambda s, g: (j, 0, gc + g)),
                pl.BlockSpec((None, 1, M_GROUP_CH), lambda s, g: (j, 0, g)),
                pl.BlockSpec((None, 1, M_STATE), lambda s, g: (j, 0, gx + g)),
                pl.BlockSpec((None, 1, M_STATE), lambda s, g: (j, 0, gc + g)),
                pl.BlockSpec((None, 1, LANES), lambda s, g: (j, 0, g)),
                pl.BlockSpec((None, 1, LANES), lambda s, g: (j, 0, g)),
                pl.BlockSpec((None, 1, M_GROUP_CH), lambda s, g: (j, 0, g)),
                pl.BlockSpec((None, 1, M_GROUP_CH), lambda s, g: (j, 0, g))]
    args = [xbc, xbc, xbc, z, dt, conv_w, conv_w, conv_w, conv_b, conv_b, conv_b,
            dtb, alog, dsk, norm_g]
    if has_h0:
        in_specs.append(pl.BlockSpec(st_block, lambda s, g: (s, 0, g, 0)))
        args.append(h0)
    out_specs = [pl.BlockSpec((seq_len, M_GROUP_CH), lambda s, g: (s, g))]
    out_shape = [jax.ShapeDtypeStruct((rows, M_INNER), BF16)]
    if emit_state:
        out_specs.append(pl.BlockSpec(st_block, lambda s, g: (s, 0, g, 0)))
        out_shape.append(jax.ShapeDtypeStruct((n_seq, 2, M_INNER, M_STATE), F32))
    scratch = [pltpu.VMEM((seq_len, M_GROUP_CH), F32),
               pltpu.VMEM((n_chunks, M_GROUP_CH, M_CHUNK), F32),
               pltpu.VMEM((seq_len, M_STATE), BF16),
               pltpu.VMEM((seq_len, M_STATE), BF16),
               pltpu.VMEM((2, seq_len, LANES), F32),
               pltpu.VMEM((2, n_chunks, M_GROUP_HEADS, M_CHUNK), F32),
               pltpu.VMEM((2, n_chunks, M_GROUP_HEADS, M_CHUNK), F32),
               pltpu.VMEM((n_chunks, M_GROUP_CH, M_CHUNK), F32),
               pltpu.VMEM((2, M_GROUP_CH, M_STATE), F32)]
    return pl.pallas_call(
        functools.partial(_ssd_kernel, seq_len=seq_len, has_h0=has_h0, emit_state=emit_state),
        grid=(n_seq, M_GROUPS),
        in_specs=in_specs,
        out_specs=out_specs,
        out_shape=out_shape,
        scratch_shapes=scratch,
        compiler_params=_cparams(2),
        name="ssd",
    )(*args)


def _gmlp_kernel(x_ref, mod_ref, g_ref, win_ref, bin_ref, ng_ref, ws_ref, bs_ref, wout_ref,
                 o_ref, gate_s, *, tm):
    x = x_ref[...]
    h = _modnorm(x, g_ref[1:2, :], mod_ref[3:4, :], mod_ref[4:5, :]).astype(BF16)
    hg = jax.nn.gelu(_dot(h, win_ref[...]) + bin_ref[...])
    u = hg[:, :G_INNER]
    v = hg[:, G_INNER:]
    v = v * lax.rsqrt(jnp.mean(v * v, axis=-1, keepdims=True) + EPS) * ng_ref[...]
    vb = v.astype(BF16)
    bs = bs_ref[...]
    for c in range(tm // G_CHUNK):
        r = slice(c * G_CHUNK, (c + 1) * G_CHUNK)
        for hd in range(G_HEADS):
            cs = slice(hd * G_HEAD_CH, (hd + 1) * G_HEAD_CH)
            sv = _dot(ws_ref[hd], vb[r, cs]) + bs[:, hd:hd + 1]
            gate_s[r, cs] = (u[r, cs] * sv).astype(BF16)
    o_ref[...] = x + mod_ref[5:6, :] * _dot(gate_s[...], wout_ref[...])


def _gmlp(x, mods, ln_g, w_in, b_in, norm_g, w_s, b_s_t, w_out, layer, j, pop):
    tm = 512
    rows = x.shape[0]
    return pl.pallas_call(
        functools.partial(_gmlp_kernel, tm=tm),
        grid=(rows // tm,),
        in_specs=[pl.BlockSpec((tm, D_MODEL), lambda t: (t, 0)),
                  _mod_spec(layer, pop, tm),
                  _ln_spec(layer),
                  _resident((D_MODEL, 2 * G_INNER), lambda t: (0, 0)),
                  _resident((None, 1, 2 * G_INNER), lambda t: (j, 0, 0)),
                  _resident((None, 1, G_INNER), lambda t: (j, 0, 0)),
                  _resident((None, G_HEADS, G_CHUNK, G_CHUNK), lambda t: (j, 0, 0, 0)),
                  _resident((None, G_CHUNK, G_HEADS), lambda t: (j, 0, 0)),
                  _resident((G_INNER, D_MODEL), lambda t: (0, 0))],
        out_specs=pl.BlockSpec((tm, D_MODEL), lambda t: (t, 0)),
        out_shape=jax.ShapeDtypeStruct((rows, D_MODEL), F32),
        scratch_shapes=[pltpu.VMEM((tm, G_INNER), BF16)],
        compiler_params=_cparams(1),
        name="gmlp",
    )(x, mods, ln_g, w_in, b_in, norm_g, w_s, b_s_t, w_out)


def _dft_tables(seq_len):
    def angles(n):
        k = np.arange(n, dtype=np.int64)
        return 2.0 * np.pi * ((k[:, None] * k[None, :]) % n).astype(np.float64) / n
    ac = angles(F_GROUP_CH)
    al = angles(seq_len)
    chan = np.concatenate([np.cos(ac), np.sin(ac)], axis=1).astype(np.float32)
    pos = np.concatenate([np.cos(al), -np.sin(al)], axis=1).astype(np.float32)
    return chan, pos


def _fnet_kernel(x_ref, mod_ref, g_ref, chan_ref, pos_ref, wout_ref, b_ref, o_ref, f_s,
                 *, seq_len):
    x = x_ref[...]
    h = _modnorm(x, g_ref[1:2, :], mod_ref[3:4, :], mod_ref[4:5, :]).astype(BF16)
    chan = chan_ref[...].astype(BF16)
    pos = pos_ref[...].astype(BF16)
    scale = 1.0 / math.sqrt(seq_len * F_GROUP_CH)
    for g in range(F_GROUPS):
        cs = slice(g * F_GROUP_CH, (g + 1) * F_GROUP_CH)
        p = _dot(h[:, cs], chan)
        stacked = jnp.concatenate([p[:, :F_GROUP_CH], p[:, F_GROUP_CH:]], axis=0).astype(BF16)
        f_s[:, cs] = (_dot(pos, stacked) * scale).astype(BF16)
    o_ref[...] = x + mod_ref[5:6, :] * (_dot(f_s[...], wout_ref[...]) + b_ref[...])


def _fnet(x, mods, ln_g, w_out, b_out, layer, j, pop, seq_len):
    rows = x.shape[0]
    chan, pos = _dft_tables(seq_len)
    return pl.pallas_call(
        functools.partial(_fnet_kernel, seq_len=seq_len),
        grid=(rows // seq_len,),
        in_specs=[pl.BlockSpec((seq_len, D_MODEL), lambda t: (t, 0)),
                  _mod_spec(layer, pop, seq_len),
                  _ln_spec(layer),
                  _resident(chan.shape, lambda t: (0, 0)),
                  _resident(pos.shape, lambda t: (0, 0)),
                  _resident((D_MODEL, D_MODEL), lambda t: (0, 0)),
                  _resident((None, 1, D_MODEL), lambda t: (j, 0, 0))],
        out_specs=pl.BlockSpec((seq_len, D_MODEL), lambda t: (t, 0)),
        out_shape=jax.ShapeDtypeStruct((rows, D_MODEL), F32),
        scratch_shapes=[pltpu.VMEM((seq_len, D_MODEL), BF16)],
        compiler_params=_cparams(1),
        name="fnet",
    )(x, mods, ln_g, jnp.asarray(chan), jnp.asarray(pos), w_out, b_out)


def _head_mean_matrix():
    i = np.arange(2 * LANES)
    return ((i[:, None] // A_HD) == (i[None, :] // A_HD)).astype(np.float32) / A_HD


def _rope_tables():
    half = A_HD // 4
    inv = (np.float32(ROPE_THETA) ** (-np.arange(half, dtype=np.float32) / np.float32(half)))
    inv = inv.astype(np.float32)
    t = np.arange(DEC_SEQ)
    lane = np.arange(LANES)
    hl = lane % A_HD
    posn = np.where(hl[None, :] < A_HD // 2, (t // GRID_W)[:, None], (t % GRID_W)[:, None])
    sub = hl % (A_HD // 2)
    ang = posn.astype(np.float32) * inv[sub % half][None, :]
    cos = np.cos(ang).astype(np.float32)
    sin = np.sin(ang).astype(np.float32)
    lower = (sub < half)[None, :]
    sin_up = np.where(lower, -sin, 0.0).astype(np.float32)
    sin_dn = np.where(lower, 0.0, sin).astype(np.float32)
    return cos, sin_up, sin_dn


def _qkv_kernel(*refs, rope, cache_seq):
    it = iter(refs)
    x_ref, mod_ref, g_ref, w_ref, qg_ref, kg_ref, avg_ref = (next(it) for _ in range(7))
    if rope:
        cos_ref, sup_ref, sdn_ref = (next(it) for _ in range(3))
    q_ref, k_ref, v_ref = (next(it) for _ in range(3))
    if cache_seq:
        kt_ref, vt_ref = next(it), next(it)
    h = _modnorm(x_ref[...], g_ref[1:2, :], mod_ref[3:4, :], mod_ref[4:5, :]).astype(BF16)
    qkv = _dot(h, w_ref[...])
    avg = avg_ref[...]
    half = A_HD // 4

    def head_norm(t, gain):
        hi, lo = _split2(t * t)
        ms = _dot(hi, avg) + _dot(lo, avg)
        return t * lax.rsqrt(ms + EPS) * gain

    def rotate(t):
        up = pltpu.roll(t, LANES - half, 1)
        dn = pltpu.roll(t, half, 1)
        return t * cos_ref[...] + up * sup_ref[...] + dn * sdn_ref[...]

    slab = 2 * LANES
    for s in range((A_Q + A_KVW) // slab):
        cs = slice(s * slab, (s + 1) * slab)
        is_q = s < A_Q // slab
        gain = qg_ref[:, cs] if is_q else kg_ref[...]
        t = head_norm(qkv[:, cs], gain)
        if rope:
            t = jnp.concatenate([rotate(t[:, :LANES]), rotate(t[:, LANES:])], axis=1)
        if is_q:
            q_ref[:, cs] = t.astype(q_ref.dtype)
        else:
            k = t
    v = qkv[:, A_Q + A_KVW:]
    k_ref[...] = k.astype(k_ref.dtype)
    v_ref[...] = v.astype(v_ref.dtype)
    if cache_seq:
        for s in range(k.shape[0] // cache_seq):
            r = slice(s * cache_seq, (s + 1) * cache_seq)
            kt_ref[s] = k[r, :].T
            vt_ref[s] = v[r, :].T


def _qkv(x, mods, ln_g, w, q_gain, k_gain, layer, j, pop):
    tm = 512
    rows = x.shape[0]
    rope = pop == 1
    cache_seq = 0 if rope else SEQ
    in_specs = [pl.BlockSpec((tm, D_MODEL), lambda t: (t, 0)),
                _mod_spec(layer, pop, tm),
                _ln_spec(layer),
                _resident((D_MODEL, A_Q + 2 * A_KVW), lambda t: (0, 0)),
                _resident((None, 1, A_Q), lambda t: (j, 0, 0)),
                _resident((None, 1, A_KVW), lambda t: (j, 0, 0)),
                _resident((2 * LANES, 2 * LANES), lambda t: (0, 0))]
    args = [x, mods, ln_g, w, q_gain, k_gain, jnp.asarray(_head_mean_matrix(), BF16)]
    if rope:
        per_seq = DEC_SEQ // tm
        in_specs += [pl.BlockSpec((tm, LANES), lambda t: (t % per_seq, 0))] * 3
        args += [jnp.asarray(a) for a in _rope_tables()]
    out_specs = [pl.BlockSpec((tm, A_Q), lambda t: (t, 0)),
                 pl.BlockSpec((tm, A_KVW), lambda t: (t, 0)),
                 pl.BlockSpec((tm, A_KVW), lambda t: (t, 0))]
    out_shape = [jax.ShapeDtypeStruct((rows, A_Q), BF16),
                 jax.ShapeDtypeStruct((rows, A_KVW), BF16),
                 jax.ShapeDtypeStruct((rows, A_KVW), BF16)]
    if cache_seq:
        per_tile = tm // cache_seq
        out_specs += [pl.BlockSpec((per_tile, A_KVW, cache_seq), lambda t: (t, 0, 0))] * 2
        out_shape += [jax.ShapeDtypeStruct((rows // cache_seq, A_KVW, cache_seq), F32)] * 2
    return pl.pallas_call(
        functools.partial(_qkv_kernel, rope=rope, cache_seq=cache_seq),
        grid=(rows // tm,),
        in_specs=in_specs,
        out_specs=out_specs,
        out_shape=out_shape,
        compiler_params=_cparams(1),
        name="qkv",
    )(*args)


def _head_place_matrices():
    m = np.zeros((A_KV, A_KVW, A_GRP * A_HD), np.float32)
    d = np.arange(A_HD)
    for j in range(A_KV):
        for g in range(A_GRP):
            m[j, j * A_HD + d, g * A_HD + d] = 1.0
    return m


def _attn_kernel(q_ref, place_ref, *refs, n_src):
    kv_refs, o_ref = refs[:2 * n_src], refs[2 * n_src]
    lane = lax.broadcasted_iota(jnp.int32, (1, A_GRP * A_HD), 1)
    scale = A_HD ** -0.5
    for j in range(A_KV):
        place = place_ref[j]
        qs = q_ref[:, j * A_GRP * A_HD:(j + 1) * A_GRP * A_HD]
        k4 = [_dot(kv_refs[2 * s][...].astype(BF16), place).astype(BF16) for s in range(n_src)]
        v4 = [_dot(kv_refs[2 * s + 1][...].astype(BF16), place).astype(BF16) for s in range(n_src)]
        acc = None
        for g in range(A_GRP):
            own = (lane >= g * A_HD) & (lane < (g + 1) * A_HD)
            qm = jnp.where(own, qs, jnp.zeros_like(qs))
            sc = [_dot_nt(qm, k4[s]) * scale for s in range(n_src)]
            mx = functools.reduce(jnp.maximum, [jnp.max(t, axis=-1, keepdims=True) for t in sc])
            p = [jnp.exp(t - mx) for t in sc]
            den = functools.reduce(jnp.add, [jnp.sum(t, axis=-1, keepdims=True) for t in p])
            pv = functools.reduce(jnp.add, [
                _dot(p[s].astype(BF16), jnp.where(own, v4[s], jnp.zeros_like(v4[s])))
                for s in range(n_src)])
            acc = pv / den if acc is None else acc + pv / den
        o_ref[:, j * A_GRP * A_HD:(j + 1) * A_GRP * A_HD] = acc.astype(o_ref.dtype)


def _attn(q, sources, seq_len, tq):
    rows = q.shape[0]
    n_batch = rows // seq_len
    per = seq_len // tq
    in_specs = [pl.BlockSpec((tq, A_Q), lambda b, t: (b * per + t, 0)),
                _resident((A_KV, A_KVW, A_GRP * A_HD), lambda b, t: (0, 0, 0))]
    args = [q, jnp.asarray(_head_place_matrices(), BF16)]
    for k, v, lk in sources:
        in_specs += [pl.BlockSpec((lk, A_KVW), lambda b, t: (b, 0))] * 2
        args += [k, v]
    return pl.pallas_call(
        functools.partial(_attn_kernel, n_src=len(sources)),
        grid=(n_batch, per),
        in_specs=in_specs,
        out_specs=pl.BlockSpec((tq, A_Q), lambda b, t: (b * per + t, 0)),
        out_shape=jax.ShapeDtypeStruct((rows, A_Q), BF16),
        compiler_params=_cparams(2),
        name="attn",
    )(*args)


def _regroup_heads(a):
    lead = a.shape[:-1]
    a = a.reshape(lead + (2, M_GROUPS, M_GROUP_HEADS))
    a = jnp.swapaxes(a, -3, -2).reshape(lead + (M_GROUPS, 2 * M_GROUP_HEADS))
    pad = [(0, 0)] * (len(lead) + 1) + [(0, LANES - 2 * M_GROUP_HEADS)]
    return jnp.pad(a, pad).reshape(lead + (M_GROUPS * LANES,))


def kernel(x_prompt, x_sample, state_ssm, cache_k, cache_v, c, c_ctx,
           ln_g, ada_w, ada_b, ff1_w_in, ff1_w_out, ff2_w_in, ff2_w_out,
           m_w_in, m_conv_w, m_conv_b, m_dt_bias, m_a_log, m_d, m_norm_g, m_w_out,
           g_w_in, g_b_in, g_norm_g, g_w_s, g_b_s, g_w_out,
           f_w_out, f_b_out,
           a_w_qkv, a_q_norm, a_k_norm, a_w_o):
    xs = [x_prompt.reshape(TP, D_MODEL), x_sample.reshape(TS, D_MODEL)]
    seq = (SEQ, DEC_SEQ)

    conds = jnp.concatenate(
        [c_ctx[None], c, jnp.zeros((COND_ROWS - 1 - DEC_BATCH, D_MODEL), F32)], axis=0)
    mods = _adaln(conds, ada_w, ada_b)

    ff_w = {(0, 0): (ff1_w_in[0].astype(BF16), ff1_w_out[0].astype(BF16))}
    n_zx = M_INNER + M_CONV_CH
    mixer_weights = ([(m_w_in, n_zx), (m_w_out, D_MODEL)],
                     [(g_w_in, 2 * G_INNER), (g_w_out, D_MODEL)],
                     [(f_w_out, D_MODEL)],
                     [(a_w_qkv, A_Q + 2 * A_KVW), (a_w_o, D_MODEL)])

    new_ssm, new_k, new_v = [], [], []
    for i in range(DEPTH):
        kind, j = i % N_MIXERS, i // N_MIXERS
        later = [(i, 1)] + ([(i + 1, 0)] if i + 1 < DEPTH else [])
        convert = []
        for layer, second in later:
            pair = (ff2_w_in, ff2_w_out) if second else (ff1_w_in, ff1_w_out)
            convert += [(pair[0], layer, 2 * FFN_HIDDEN), (pair[1], layer, D_MODEL)]
        convert += [(w, j, n_cols) for w, n_cols in mixer_weights[kind]]
        w_in, w_out = ff_w[(i, 0)]
        xs[0], cast = _ffn(xs[0], mods, ln_g, w_in, w_out, i, False, 0, convert=convert)
        for n, key in enumerate(later):
            ff_w[key] = (cast[2 * n], cast[2 * n + 1])
        mixer_w = cast[2 * len(later):]
        xs[1], _ = _ffn(xs[1], mods, ln_g, w_in, w_out, i, False, 1)
        mix = [None, None]
        if kind == 0:
            w_zx, w_out = mixer_w
            w_dt = _regroup_heads(m_w_in[j, :, n_zx:]).astype(BF16)
            dtb = _regroup_heads(m_dt_bias.reshape(-1, 1, 2 * M_HEADS))
            alog = _regroup_heads(m_a_log.reshape(-1, 1, 2 * M_HEADS))
            dsk = jnp.repeat(m_d, M_HEADDIM, axis=-1)[:, None, :]
            conv_b = m_conv_b[:, None, :]
            norm_g = m_norm_g[:, None, :]
            h0 = state_ssm[:, j].reshape(DEC_BATCH, 2, M_INNER, M_STATE)
            for p in range(2):
                z, xbc, dt = _modlinear(xs[p], mods, ln_g, [w_zx, w_dt], i, p,
                                        ((M_INNER, M_CONV_CH), (M_GROUPS * LANES,)))
                res = _ssd(xbc, z, dt, m_conv_w, conv_b, dtb, alog, dsk, norm_g,
                           h0 if p == 1 else None, j, seq[p], emit_state=(p == 0))
                if p == 0:
                    new_ssm.append(res[1].reshape(BATCH, 2, M_HEADS, M_HEADDIM, M_STATE))
                mix[p] = (res[0], w_out)
        elif kind == 1:
            w_in, w_out = mixer_w
            w_s = g_w_s.astype(BF16)
            b_s_t = jnp.swapaxes(g_b_s, -1, -2)
            xs = [_gmlp(xs[p], mods, ln_g, w_in, g_b_in[:, None, :], g_norm_g[:, None, :],
                        w_s, b_s_t, w_out, i, j, p) for p in range(2)]
        elif kind == 2:
            w_out, = mixer_w
            xs = [_fnet(xs[p], mods, ln_g, w_out, f_b_out[:, None, :], i, j, p, seq[p])
                  for p in range(2)]
        else:
            w_qkv, w_o = mixer_w
            q_gain = jnp.tile(a_q_norm, (1, A_HEADS))[:, None, :]
            k_gain = jnp.tile(a_k_norm, (1, A_KV))[:, None, :]
            qp, kp, vp, kt, vt = _qkv(xs[0], mods, ln_g, w_qkv, q_gain, k_gain, i, j, 0)
            for cache, t in ((new_k, kt), (new_v, vt)):
                cache.append(jnp.transpose(t.reshape(BATCH, A_KV, A_HD, SEQ), (0, 3, 1, 2)))
            op = _attn(qp, [(kp, vp, SEQ)], SEQ, SEQ)
            ql, kl, vl = _qkv(xs[1], mods, ln_g, w_qkv, q_gain, k_gain, i, j, 1)
            kc = cache_k[:, j].reshape(DEC_BATCH * PAST_LEN, A_KVW)
            vc = cache_v[:, j].reshape(DEC_BATCH * PAST_LEN, A_KVW)
            ol = _attn(ql, [(kc, vc, PAST_LEN), (kl, vl, DEC_SEQ)], DEC_SEQ, 256)
            mix = [(op, w_o), (ol, w_o)]
        w_in, w_out = ff_w[(i, 1)]
        xs = [_ffn(xs[p], mods, ln_g, w_in, w_out, i, True, p, mix[p])[0] for p in range(2)]

    return (xs[0].reshape(BATCH, SEQ, D_MODEL),
            xs[1].reshape(DEC_BATCH, DEC_SEQ, D_MODEL),
            jnp.stack(new_ssm, axis=1),
            jnp.stack(new_k, axis=1),
            jnp.stack(new_v, axis=1))
```

```python
import functools
import math

import numpy as np
import jax
import jax.numpy as jnp
from jax import lax
from jax.experimental import pallas as pl
from jax.experimental.pallas import tpu as pltpu

F32 = jnp.float32
BF16 = jnp.bfloat16

D_MODEL = 1024
BATCH = 32
SEQ = 256
DEPTH = 4
DEC_BATCH = 2
DEC_SEQ = 1024
PAST_LEN = 256
GRID_W = 64
N_MIXERS = 4
N_MOD = 9
EPS = 1e-6
FFN_HIDDEN = 2816

M_INNER = 2 * D_MODEL
M_HEADDIM = 64
M_HEADS = M_INNER // M_HEADDIM
M_GROUPS = 4
M_STATE = 128
M_CONV = 3
M_CHUNK = 128
M_CONV_CH = M_INNER + 2 * M_GROUPS * M_STATE
M_GROUP_CH = M_INNER // M_GROUPS
M_GROUP_HEADS = M_HEADS // M_GROUPS

G_CHUNK = 128
G_INNER = 2 * D_MODEL
G_HEADS = 8
G_HEAD_CH = G_INNER // G_HEADS

F_GROUPS = 4
F_GROUP_CH = D_MODEL // F_GROUPS

A_HEADS = 16
A_KV = 4
A_HD = 64
A_GRP = A_HEADS // A_KV
A_Q = A_HEADS * A_HD
A_KVW = A_KV * A_HD
ROPE_THETA = 10000.0

LANES = 128
COND_ROWS = 8
TP = BATCH * SEQ
TS = DEC_BATCH * DEC_SEQ
VMEM_LIMIT = 56 * 2**20


def _cparams(n_grid):
    return pltpu.CompilerParams(dimension_semantics=("arbitrary",) * n_grid,
                                vmem_limit_bytes=VMEM_LIMIT)


def _resident(block_shape, index_map):
    return pl.BlockSpec(block_shape, index_map, pipeline_mode=pl.Buffered(1))


def _dot(a, b):
    return jnp.dot(a, b, preferred_element_type=F32)


def _dot_nt(a, b):
    return lax.dot_general(a, b, (((1,), (1,)), ((), ())), preferred_element_type=F32)


def _silu(x):
    return x * jax.nn.sigmoid(x)


def _softplus(x):
    return jnp.maximum(x, 0.0) + jnp.log1p(jnp.exp(-jnp.abs(x)))


def _modnorm(x, g, shift, scale):
    y = x * lax.rsqrt(jnp.mean(x * x, axis=-1, keepdims=True) + EPS) * g
    return y * (1.0 + scale) + shift


def _split3(v):
    hi = v.astype(BF16)
    r1 = v - hi.astype(F32)
    mid = r1.astype(BF16)
    lo = (r1 - mid.astype(F32)).astype(BF16)
    return hi, mid, lo


def _split2(v):
    hi = v.astype(BF16)
    return hi, (v - hi.astype(F32)).astype(BF16)


def _adaln_kernel(c_ref, w_ref, b_ref, o_ref):
    s = _silu(c_ref[...]).astype(BF16)
    o_ref[...] = _dot(s, w_ref[...].astype(BF16)) + b_ref[...]


def _adaln(conds, ada_w, ada_b):
    tn = 2304
    n = N_MOD * D_MODEL
    out = pl.pallas_call(
        _adaln_kernel,
        grid=(DEPTH, n // tn),
        in_specs=[pl.BlockSpec((COND_ROWS, D_MODEL), lambda i, j: (0, 0)),
                  pl.BlockSpec((None, D_MODEL, tn), lambda i, j: (i, 0, j)),
                  pl.BlockSpec((None, 1, tn), lambda i, j: (i, 0, j))],
        out_specs=pl.BlockSpec((None, COND_ROWS, tn), lambda i, j: (i, 0, j)),
        out_shape=jax.ShapeDtypeStruct((DEPTH, COND_ROWS, n), F32),
        compiler_params=_cparams(2),
        name="adaln",
    )(conds, ada_w, ada_b.reshape(DEPTH, 1, n))
    return out.reshape(DEPTH, COND_ROWS, N_MOD, D_MODEL)


def _mod_spec(layer, pop, tm):
    if pop == 0:
        return pl.BlockSpec((None, None, N_MOD, D_MODEL), lambda t, *_: (layer, 0, 0, 0))
    return pl.BlockSpec((None, None, N_MOD, D_MODEL),
                        lambda t, *_: (layer, 1 + (t * tm) // DEC_SEQ, 0, 0))


def _ln_spec(layer):
    return pl.BlockSpec((None, 3, D_MODEL), lambda t, *_: (layer, 0, 0))


MXU_COLS = 256
FFN_PIECES = tuple((lo, min(lo + 3 * MXU_COLS, FFN_HIDDEN))
                   for lo in range(0, FFN_HIDDEN, 3 * MXU_COLS))


def _ffn_kernel(*refs, k0, gk, has_mix, n_convert):
    it = iter(refs)
    if has_mix:
        y_ref, wmix_ref = next(it), next(it)
    x_ref, mod_ref, g_ref, win_hbm, wout_hbm = (next(it) for _ in range(5))
    src_refs = [next(it) for _ in range(n_convert)]
    o_ref = next(it)
    dst_refs = [next(it) for _ in range(n_convert)]
    win_ref, wout_ref, sem = (next(it) for _ in range(3))
    first = pl.program_id(0) == 0

    def piece_copies(p):
        lo, hi = FFN_PIECES[p]
        cols = pl.ds(lo, hi - lo)
        up_cols = pl.ds(FFN_HIDDEN + lo, hi - lo)
        return (pltpu.make_async_copy(win_hbm.at[:, cols], win_ref.at[:, cols], sem.at[p, 0]),
                pltpu.make_async_copy(win_hbm.at[:, up_cols], win_ref.at[:, up_cols], sem.at[p, 1]),
                pltpu.make_async_copy(wout_hbm.at[cols, :], wout_ref.at[cols, :], sem.at[p, 2]))

    @pl.when(first)
    def _():
        for p in range(len(FFN_PIECES)):
            for cp in piece_copies(p):
                cp.start()

    x = x_ref[...]
    if has_mix:
        x = x + mod_ref[5:6, :] * _dot(y_ref[...], wmix_ref[...])
    h = _modnorm(x, g_ref[gk:gk + 1, :], mod_ref[k0:k0 + 1, :], mod_ref[k0 + 1:k0 + 2, :])
    h = h.astype(BF16)

    def half_step(wait_for_weights):
        o = None
        for p, (lo, hi) in enumerate(FFN_PIECES):
            if wait_for_weights:
                for cp in piece_copies(p):
                    cp.wait()
            gate = _dot(h, win_ref[:, lo:hi])
            up = _dot(h, win_ref[:, FFN_HIDDEN + lo:FFN_HIDDEN + hi])
            part = _dot((_silu(gate) * up).astype(BF16), wout_ref[lo:hi, :])
            o = part if o is None else o + part
        o_ref[...] = x + 0.5 * mod_ref[k0 + 2:k0 + 3, :] * o

    pl.when(first)(lambda: half_step(True))
    pl.when(jnp.logical_not(first))(lambda: half_step(False))
    for src, dst in zip(src_refs, dst_refs):
        dst[...] = src[...].astype(BF16)


def _ffn(x, mods, ln_g, w_in, w_out, layer, second, pop, mix=None, convert=()):
    tm = 512
    rows = x.shape[0]
    steps = rows // tm
    in_specs = [pl.BlockSpec((tm, D_MODEL), lambda t: (t, 0)),
                _mod_spec(layer, pop, tm),
                _ln_spec(layer),
                pl.BlockSpec(memory_space=pl.ANY),
                pl.BlockSpec(memory_space=pl.ANY)]
    args = [x, mods, ln_g, w_in, w_out]
    if mix is not None:
        y, w_mix = mix
        k = y.shape[1]
        in_specs = [pl.BlockSpec((tm, k), lambda t: (t, 0)),
                    _resident((k, D_MODEL), lambda t: (0, 0))] + in_specs
        args = [y, w_mix] + args
    out_specs = [pl.BlockSpec((tm, D_MODEL), lambda t: (t, 0))]
    out_shape = [jax.ShapeDtypeStruct((rows, D_MODEL), F32)]
    for src, index, n_rows, n_cols in convert:
        slab = n_rows // steps
        in_specs.append(pl.BlockSpec((None, slab, n_cols), lambda t, index=index: (index, t, 0)))
        args.append(src)
        out_specs.append(pl.BlockSpec((slab, n_cols), lambda t: (t, 0)))
        out_shape.append(jax.ShapeDtypeStruct((n_rows, n_cols), BF16))
    res = pl.pallas_call(
        functools.partial(_ffn_kernel, k0=6 if second else 0, gk=2 if second else 0,
                          has_mix=mix is not None, n_convert=len(convert)),
        grid=(steps,),
        in_specs=in_specs,
        out_specs=out_specs,
        out_shape=out_shape,
        scratch_shapes=[pltpu.VMEM((D_MODEL, 2 * FFN_HIDDEN), BF16),
                        pltpu.VMEM((FFN_HIDDEN, D_MODEL), BF16),
                        pltpu.SemaphoreType.DMA((len(FFN_PIECES), 3))],
        compiler_params=_cparams(1),
        name="ffn",
    )(*args)
    return res[0], list(res[1:])


def _modlinear_kernel(x_ref, mod_ref, g_ref, *refs, widths, transposed):
    w_refs, o_refs = refs[:len(widths)], iter(refs[len(widths):])
    h = _modnorm(x_ref[...], g_ref[1:2, :], mod_ref[3:4, :], mod_ref[4:5, :]).astype(BF16)
    for w_ref, group, w_is_t in zip(w_refs, widths, transposed):
        off = 0
        for wd in group:
            if w_is_t:
                next(o_refs)[...] = _dot_nt(h, w_ref[off:off + wd, :])
            else:
                next(o_refs)[...] = _dot(h, w_ref[:, off:off + wd])
            off += wd


def _modlinear(x, mods, ln_g, weights, layer, pop, widths, transposed):
    tm = 512
    rows = x.shape[0]
    flat = [wd for group in widths for wd in group]
    return pl.pallas_call(
        functools.partial(_modlinear_kernel, widths=widths, transposed=transposed),
        grid=(rows // tm,),
        in_specs=[pl.BlockSpec((tm, D_MODEL), lambda t: (t, 0)),
                  _mod_spec(layer, pop, tm),
                  _ln_spec(layer)]
                 + [_resident(w.shape, lambda t: (0, 0)) for w in weights],
        out_specs=[pl.BlockSpec((tm, wd), lambda t: (t, 0)) for wd in flat],
        out_shape=[jax.ShapeDtypeStruct((rows, wd), F32) for wd in flat],
        compiler_params=_cparams(1),
        name="modlinear",
    )(x, mods, ln_g, *weights)


SSD_ROWS_PER_STEP = 2048


def _ssd_kernel(*refs, seq_len, has_h0, emit_state, gps):
    it = iter(refs)
    x_ref, b_ref, c_ref, z_ref, dt_ref = (next(it) for _ in range(5))
    cwx_ref, cwb_ref, cwc_ref, cbx_ref, cbb_ref, cbc_ref = (next(it) for _ in range(6))
    dtb_ref, alog_ref, dsk_ref, ng_ref = (next(it) for _ in range(4))
    h0_ref = next(it) if has_h0 else None
    y_ref = next(it)
    st_ref = next(it) if emit_state else None
    scratch = [next(it) for _ in range(9)]
    n_chunks = seq_len // M_CHUNK

    row = lax.broadcasted_iota(jnp.int32, (M_CHUNK, M_CHUNK), 0)
    col = lax.broadcasted_iota(jnp.int32, (M_CHUNK, M_CHUNK), 1)
    keeps = (col <= row, col >= row)
    tris = [jnp.where(k, 1.0, 0.0).astype(BF16) for k in keeps]

    def conv_silu(ref, w_ref, bias_ref):
        v = ref[...]
        t = lax.broadcasted_iota(jnp.int32, v.shape, 0)
        prev = jnp.where(t == 0, 0.0, pltpu.roll(v, 1, 0))
        nxt = jnp.where(t == seq_len - 1, 0.0, pltpu.roll(v, seq_len - 1, 0))
        w = w_ref[...]
        return _silu(prev * w[0:1, :] + v * w[1:2, :] + nxt * w[2:3, :] + bias_ref[...])

    def head_rows(v8):
        return jnp.concatenate(
            [jnp.broadcast_to(v8[r:r + 1, :], (M_HEADDIM, v8.shape[1]))
             for r in range(M_GROUP_HEADS)], axis=0)

    def make_group(q):
        ch = pl.ds(q * M_GROUP_CH, M_GROUP_CH)
        st = pl.ds(q * M_STATE, M_STATE)
        hd = pl.ds(q * LANES, LANES)
        xs_s, xt_s, b_s, c_s, acum_s, a8_s, dt8_s, yt_s, h_s = (s.at[q] for s in scratch)

        def prologue():
            xs = conv_silu(x_ref.at[:, ch], cwx_ref.at[:, ch], cbx_ref.at[:, ch])
            xs_s[...] = xs
            b_s[...] = conv_silu(b_ref.at[:, st], cwb_ref.at[:, st], cbb_ref.at[:, st]).astype(BF16)
            c_s[...] = conv_silu(c_ref.at[:, st], cwc_ref.at[:, st], cbc_ref.at[:, st]).astype(BF16)
            dt = _softplus(dt_ref[:, hd] + dtb_ref[:, hd])
            dta = dt * (-jnp.exp(alog_ref[:, hd]))
            for c in range(n_chunks):
                r = slice(c * M_CHUNK, (c + 1) * M_CHUNK)
                dt_t = dt[r, :].T
                by_time = _split3(dta[r, :])
                by_head = _split3(dta[r, :].T)
                for d in range(2):
                    hl = slice(M_GROUP_HEADS * d, M_GROUP_HEADS * (d + 1))
                    acum_s[d, r, :] = functools.reduce(
                        jnp.add, [_dot(tris[d], p) for p in by_time])
                    acum_t = functools.reduce(jnp.add, [_dot_nt(p, tris[d]) for p in by_head])
                    a8_s[d, c] = acum_t[hl, :]
                    dt8_s[d, c] = dt_t[hl, :]
                for k in range(M_GROUP_CH // LANES):
                    xt_s[c, k * LANES:(k + 1) * LANES, :] = xs[r, k * LANES:(k + 1) * LANES].T
            yt_s[...] = jnp.zeros(yt_s.shape, F32)
            if has_h0:
                h_s[...] = h0_ref[:, ch, :]
            else:
                h_s[...] = jnp.zeros(h_s.shape, F32)

        def chunk_step(c, d):
            start = c * M_CHUNK
            rows = pl.ds(start if isinstance(c, int) else pl.multiple_of(start, M_CHUNK), M_CHUNK)
            keep = keeps[d]
            end = M_CHUNK - 1 if d == 0 else 0
            acum = acum_s[d, rows, :]
            a8 = a8_s[d, c]
            dt8 = dt8_s[d, c]
            a_end8 = jnp.broadcast_to(a8[:, end:end + 1], a8.shape)
            w8 = jnp.exp(a_end8 - a8) * dt8
            bc = b_s[rows, :]
            cc = c_s[rows, :]
            cb = _dot_nt(cc, bc)
            h_in = h_s[d]
            h_s[d] = (h_in * head_rows(jnp.exp(a_end8))
                      + _dot((xt_s[c] * head_rows(w8)).astype(BF16), bc))
            yt_s[c] += _dot_nt(h_in.astype(BF16), cc) * head_rows(jnp.exp(a8))
            for r in range(M_GROUP_HEADS):
                ln = M_GROUP_HEADS * d + r
                hs = slice(r * M_HEADDIM, (r + 1) * M_HEADDIM)
                seg = jnp.broadcast_to(acum[:, ln:ln + 1], (M_CHUNK, M_CHUNK)) - a8[r:r + 1, :]
                decay = jnp.exp(jnp.where(keep, seg, -jnp.inf))
                m = (cb * decay * dt8[r:r + 1, :]).astype(BF16)
                yt_s[c, hs, :] += _dot_nt(xt_s[c, hs, :].astype(BF16), m)

        def epilogue():
            for c in range(n_chunks):
                r = slice(c * M_CHUNK, (c + 1) * M_CHUNK)
                for k in range(M_GROUP_CH // LANES):
                    cs = slice(k * LANES, (k + 1) * LANES)
                    skip = dsk_ref[:, pl.ds(q * M_GROUP_CH + k * LANES, LANES)]
                    xs_s[r, cs] = xs_s[r, cs] * skip + yt_s[c, cs, :].T
            y = xs_s[...] * _silu(z_ref[:, ch])
            y = y * lax.rsqrt(jnp.mean(y * y, axis=-1, keepdims=True) + EPS) * ng_ref[:, ch]
            y_ref[:, ch] = y.astype(BF16)
            if emit_state:
                st_ref[:, ch, :] = h_s[...]

        return prologue, chunk_step, epilogue

    groups = [make_group(q) for q in range(gps)]
    for prologue, _, _ in groups:
        prologue()

    def both_directions(i):
        for _, chunk_step, _ in groups:
            chunk_step(i, 0)
        for _, chunk_step, _ in groups:
            chunk_step(n_chunks - 1 - i, 1)

    if n_chunks <= 2:
        for i in range(n_chunks):
            both_directions(i)
    else:
        def body(i, carry):
            both_directions(i)
            return carry
        lax.fori_loop(0, n_chunks, body, 0)

    for _, _, epilogue in groups:
        epilogue()


def _ssd(xbc, z, dt, conv_w, conv_b, dtb, alog, dsk, norm_g, h0, j, seq_len, emit_state):
    rows = xbc.shape[0]
    n_seq = rows // seq_len
    gps = min(M_GROUPS, SSD_ROWS_PER_STEP // seq_len)
    ch, st, hd = gps * M_GROUP_CH, gps * M_STATE, gps * LANES
    gx = M_INNER // st
    gc = gx + M_GROUPS * M_STATE // st
    has_h0 = h0 is not None
    n_chunks = seq_len // M_CHUNK
    st_block = (None, 2, ch, M_STATE)

    in_specs = [pl.BlockSpec((seq_len, ch), lambda s, g: (s, g)),
                pl.BlockSpec((seq_len, st), lambda s, g: (s, gx + g)),
                pl.BlockSpec((seq_len, st), lambda s, g: (s, gc + g)),
                pl.BlockSpec((seq_len, ch), lambda s, g: (s, g)),
                pl.BlockSpec((seq_len, hd), lambda s, g: (s, g)),
                pl.BlockSpec((None, M_CONV, ch), lambda s, g: (j, 0, g)),
                pl.BlockSpec((None, M_CONV, st), lambda s, g: (j, 0, gx + g)),
                pl.BlockSpec((None, M_CONV, st), lambda s, g: (j, 0, gc + g)),
                pl.BlockSpec((None, 1, ch), lambda s, g: (j, 0, g)),
                pl.BlockSpec((None, 1, st), lambda s, g: (j, 0, gx + g)),
                pl.BlockSpec((None, 1, st), lambda s, g: (j, 0, gc + g)),
                pl.BlockSpec((None, 1, hd), lambda s, g: (j, 0, g)),
                pl.BlockSpec((None, 1, hd), lambda s, g: (j, 0, g)),
                pl.BlockSpec((None, 1, ch), lambda s, g: (j, 0, g)),
                pl.BlockSpec((None, 1, ch), lambda s, g: (j, 0, g))]
    args = [xbc, xbc, xbc, z, dt, conv_w, conv_w, conv_w, conv_b, conv_b, conv_b,
            dtb, alog, dsk, norm_g]
    if has_h0:
        in_specs.append(pl.BlockSpec(st_block, lambda s, g: (s, 0, g, 0)))
        args.append(h0)
    out_specs = [pl.BlockSpec((seq_len, ch), lambda s, g: (s, g))]
    out_shape = [jax.ShapeDtypeStruct((rows, M_INNER), BF16)]
    if emit_state:
        out_specs.append(pl.BlockSpec(st_block, lambda s, g: (s, 0, g, 0)))
        out_shape.append(jax.ShapeDtypeStruct((n_seq, 2, M_INNER, M_STATE), F32))
    scratch = [pltpu.VMEM((gps, seq_len, M_GROUP_CH), F32),
               pltpu.VMEM((gps, n_chunks, M_GROUP_CH, M_CHUNK), F32),
               pltpu.VMEM((gps, seq_len, M_STATE), BF16),
               pltpu.VMEM((gps, seq_len, M_STATE), BF16),
               pltpu.VMEM((gps, 2, seq_len, LANES), F32),
               pltpu.VMEM((gps, 2, n_chunks, M_GROUP_HEADS, M_CHUNK), F32),
               pltpu.VMEM((gps, 2, n_chunks, M_GROUP_HEADS, M_CHUNK), F32),
               pltpu.VMEM((gps, n_chunks, M_GROUP_CH, M_CHUNK), F32),
               pltpu.VMEM((gps, 2, M_GROUP_CH, M_STATE), F32)]
    return pl.pallas_call(
        functools.partial(_ssd_kernel, seq_len=seq_len, has_h0=has_h0, emit_state=emit_state,
                          gps=gps),
        grid=(n_seq, M_GROUPS // gps),
        in_specs=in_specs,
        out_specs=out_specs,
        out_shape=out_shape,
        scratch_shapes=scratch,
        compiler_params=_cparams(2),
        name="ssd",
    )(*args)


def _gmlp_kernel(x_ref, mod_ref, g_ref, win_ref, bin_ref, ng_ref, ws_ref, bs_ref, wout_ref,
                 o_ref, gate_s, *, tm):
    x = x_ref[...]
    h = _modnorm(x, g_ref[1:2, :], mod_ref[3:4, :], mod_ref[4:5, :]).astype(BF16)
    hg = jax.nn.gelu(_dot(h, win_ref[...]) + bin_ref[...])
    u = hg[:, :G_INNER]
    v = hg[:, G_INNER:]
    v = v * lax.rsqrt(jnp.mean(v * v, axis=-1, keepdims=True) + EPS) * ng_ref[...]
    vb = v.astype(BF16)
    bs = bs_ref[...]
    for c in range(tm // G_CHUNK):
        r = slice(c * G_CHUNK, (c + 1) * G_CHUNK)
        for hd in range(G_HEADS):
            cs = slice(hd * G_HEAD_CH, (hd + 1) * G_HEAD_CH)
            sv = _dot(ws_ref[hd], vb[r, cs]) + bs[:, hd:hd + 1]
            gate_s[r, cs] = (u[r, cs] * sv).astype(BF16)
    o_ref[...] = x + mod_ref[5:6, :] * _dot(gate_s[...], wout_ref[...])


def _gmlp(x, mods, ln_g, w_in, b_in, norm_g, w_s, b_s_t, w_out, layer, j, pop):
    tm = 512
    rows = x.shape[0]
    return pl.pallas_call(
        functools.partial(_gmlp_kernel, tm=tm),
        grid=(rows // tm,),
        in_specs=[pl.BlockSpec((tm, D_MODEL), lambda t: (t, 0)),
                  _mod_spec(layer, pop, tm),
                  _ln_spec(layer),
                  _resident((D_MODEL, 2 * G_INNER), lambda t: (0, 0)),
                  _resident((None, 1, 2 * G_INNER), lambda t: (j, 0, 0)),
                  _resident((None, 1, G_INNER), lambda t: (j, 0, 0)),
                  _resident((None, G_HEADS, G_CHUNK, G_CHUNK), lambda t: (j, 0, 0, 0)),
                  _resident((None, G_CHUNK, G_HEADS), lambda t: (j, 0, 0)),
                  _resident((G_INNER, D_MODEL), lambda t: (0, 0))],
        out_specs=pl.BlockSpec((tm, D_MODEL), lambda t: (t, 0)),
        out_shape=jax.ShapeDtypeStruct((rows, D_MODEL), F32),
        scratch_shapes=[pltpu.VMEM((tm, G_INNER), BF16)],
        compiler_params=_cparams(1),
        name="gmlp",
    )(x, mods, ln_g, w_in, b_in, norm_g, w_s, b_s_t, w_out)


def _dft_tables(seq_len):
    def angles(n):
        k = np.arange(n, dtype=np.int64)
        return 2.0 * np.pi * ((k[:, None] * k[None, :]) % n).astype(np.float64) / n
    ac = angles(F_GROUP_CH)
    al = angles(seq_len)
    chan = np.concatenate([np.cos(ac), np.sin(ac)], axis=1).astype(np.float32)
    pos = np.concatenate([np.cos(al), -np.sin(al)], axis=1).astype(np.float32)
    return chan, pos


def _fnet_kernel(x_ref, mod_ref, g_ref, chan_ref, pos_ref, wout_ref, b_ref, o_ref, f_s,
                 *, seq_len):
    x = x_ref[...]
    h = _modnorm(x, g_ref[1:2, :], mod_ref[3:4, :], mod_ref[4:5, :]).astype(BF16)
    chan = chan_ref[...].astype(BF16)
    pos = pos_ref[...].astype(BF16)
    scale = 1.0 / math.sqrt(seq_len * F_GROUP_CH)
    for g in range(F_GROUPS):
        cs = slice(g * F_GROUP_CH, (g + 1) * F_GROUP_CH)
        p = _dot(h[:, cs], chan)
        stacked = jnp.concatenate([p[:, :F_GROUP_CH], p[:, F_GROUP_CH:]], axis=0).astype(BF16)
        f_s[:, cs] = (_dot(pos, stacked) * scale).astype(BF16)
    o_ref[...] = x + mod_ref[5:6, :] * (_dot(f_s[...], wout_ref[...]) + b_ref[...])


def _fnet(x, mods, ln_g, w_out, b_out, layer, j, pop, seq_len):
    rows = x.shape[0]
    chan, pos = _dft_tables(seq_len)
    return pl.pallas_call(
        functools.partial(_fnet_kernel, seq_len=seq_len),
        grid=(rows // seq_len,),
        in_specs=[pl.BlockSpec((seq_len, D_MODEL), lambda t: (t, 0)),
                  _mod_spec(layer, pop, seq_len),
                  _ln_spec(layer),
                  _resident(chan.shape, lambda t: (0, 0)),
                  _resident(pos.shape, lambda t: (0, 0)),
                  _resident((D_MODEL, D_MODEL), lambda t: (0, 0)),
                  _resident((None, 1, D_MODEL), lambda t: (j, 0, 0))],
        out_specs=pl.BlockSpec((seq_len, D_MODEL), lambda t: (t, 0)),
        out_shape=jax.ShapeDtypeStruct((rows, D_MODEL), F32),
        scratch_shapes=[pltpu.VMEM((seq_len, D_MODEL), BF16)],
        compiler_params=_cparams(1),
        name="fnet",
    )(x, mods, ln_g, jnp.asarray(chan), jnp.asarray(pos), w_out, b_out)


def _head_mean_matrix():
    i = np.arange(2 * LANES)
    return ((i[:, None] // A_HD) == (i[None, :] // A_HD)).astype(np.float32) / A_HD


def _rope_tables():
    half = A_HD // 4
    inv = (np.float32(ROPE_THETA) ** (-np.arange(half, dtype=np.float32) / np.float32(half)))
    inv = inv.astype(np.float32)
    t = np.arange(DEC_SEQ)
    lane = np.arange(LANES)
    hl = lane % A_HD
    posn = np.where(hl[None, :] < A_HD // 2, (t // GRID_W)[:, None], (t % GRID_W)[:, None])
    sub = hl % (A_HD // 2)
    ang = posn.astype(np.float32) * inv[sub % half][None, :]
    cos = np.cos(ang).astype(np.float32)
    sin = np.sin(ang).astype(np.float32)
    lower = (sub < half)[None, :]
    sin_up = np.where(lower, -sin, 0.0).astype(np.float32)
    sin_dn = np.where(lower, 0.0, sin).astype(np.float32)
    return cos, sin_up, sin_dn


def _qkv_kernel(*refs, rope, cache_seq):
    it = iter(refs)
    x_ref, mod_ref, g_ref, w_ref, qg_ref, kg_ref, avg_ref = (next(it) for _ in range(7))
    if rope:
        cos_ref, sup_ref, sdn_ref = (next(it) for _ in range(3))
    q_ref, k_ref, v_ref = (next(it) for _ in range(3))
    if cache_seq:
        kt_ref, vt_ref = next(it), next(it)
    h = _modnorm(x_ref[...], g_ref[1:2, :], mod_ref[3:4, :], mod_ref[4:5, :]).astype(BF16)
    qkv = _dot(h, w_ref[...])
    avg = avg_ref[...]
    half = A_HD // 4

    def head_norm(t, gain):
        hi, lo = _split2(t * t)
        ms = _dot(hi, avg) + _dot(lo, avg)
        return t * lax.rsqrt(ms + EPS) * gain

    def rotate(t):
        up = pltpu.roll(t, LANES - half, 1)
        dn = pltpu.roll(t, half, 1)
        return t * cos_ref[...] + up * sup_ref[...] + dn * sdn_ref[...]

    slab = 2 * LANES
    for s in range((A_Q + A_KVW) // slab):
        cs = slice(s * slab, (s + 1) * slab)
        is_q = s < A_Q // slab
        gain = qg_ref[:, cs] if is_q else kg_ref[...]
        t = head_norm(qkv[:, cs], gain)
        if rope:
            t = jnp.concatenate([rotate(t[:, :LANES]), rotate(t[:, LANES:])], axis=1)
        if is_q:
            q_ref[:, cs] = t.astype(q_ref.dtype)
        else:
            k = t
    v = qkv[:, A_Q + A_KVW:]
    k_ref[...] = k.astype(k_ref.dtype)
    v_ref[...] = v.astype(v_ref.dtype)
    if cache_seq:
        for s in range(k.shape[0] // cache_seq):
            r = slice(s * cache_seq, (s + 1) * cache_seq)
            kt_ref[s] = k[r, :].T
            vt_ref[s] = v[r, :].T


def _qkv(x, mods, ln_g, w, q_gain, k_gain, layer, j, pop):
    tm = 512
    rows = x.shape[0]
    rope = pop == 1
    cache_seq = 0 if rope else SEQ
    in_specs = [pl.BlockSpec((tm, D_MODEL), lambda t: (t, 0)),
                _mod_spec(layer, pop, tm),
                _ln_spec(layer),
                _resident((D_MODEL, A_Q + 2 * A_KVW), lambda t: (0, 0)),
                _resident((None, 1, A_Q), lambda t: (j, 0, 0)),
                _resident((None, 1, A_KVW), lambda t: (j, 0, 0)),
                _resident((2 * LANES, 2 * LANES), lambda t: (0, 0))]
    args = [x, mods, ln_g, w, q_gain, k_gain, jnp.asarray(_head_mean_matrix(), BF16)]
    if rope:
        per_seq = DEC_SEQ // tm
        in_specs += [pl.BlockSpec((tm, LANES), lambda t: (t % per_seq, 0))] * 3
        args += [jnp.asarray(a) for a in _rope_tables()]
    out_specs = [pl.BlockSpec((tm, A_Q), lambda t: (t, 0)),
                 pl.BlockSpec((tm, A_KVW), lambda t: (t, 0)),
                 pl.BlockSpec((tm, A_KVW), lambda t: (t, 0))]
    out_shape = [jax.ShapeDtypeStruct((rows, A_Q), BF16),
                 jax.ShapeDtypeStruct((rows, A_KVW), BF16),
                 jax.ShapeDtypeStruct((rows, A_KVW), BF16)]
    if cache_seq:
        per_tile = tm // cache_seq
        out_specs += [pl.BlockSpec((per_tile, A_KVW, cache_seq), lambda t: (t, 0, 0))] * 2
        out_shape += [jax.ShapeDtypeStruct((rows // cache_seq, A_KVW, cache_seq), F32)] * 2
    return pl.pallas_call(
        functools.partial(_qkv_kernel, rope=rope, cache_seq=cache_seq),
        grid=(rows // tm,),
        in_specs=in_specs,
        out_specs=out_specs,
        out_shape=out_shape,
        compiler_params=_cparams(1),
        name="qkv",
    )(*args)


def _head_place_matrices():
    m = np.zeros((A_KV, A_KVW, A_GRP * A_HD), np.float32)
    d = np.arange(A_HD)
    for j in range(A_KV):
        for g in range(A_GRP):
            m[j, j * A_HD + d, g * A_HD + d] = 1.0
    return m


def _attn_kernel(q_ref, place_ref, *refs, n_src):
    kv_refs, o_ref = refs[:2 * n_src], refs[2 * n_src]
    lane = lax.broadcasted_iota(jnp.int32, (1, A_GRP * A_HD), 1)
    scale = A_HD ** -0.5
    for j in range(A_KV):
        place = place_ref[j]
        qs = q_ref[:, j * A_GRP * A_HD:(j + 1) * A_GRP * A_HD]
        k4 = [_dot(kv_refs[2 * s][...].astype(BF16), place).astype(BF16) for s in range(n_src)]
        v4 = [_dot(kv_refs[2 * s + 1][...].astype(BF16), place).astype(BF16) for s in range(n_src)]
        acc = None
        for g in range(A_GRP):
            own = (lane >= g * A_HD) & (lane < (g + 1) * A_HD)
            qm = jnp.where(own, qs, jnp.zeros_like(qs))
            sc = [_dot_nt(qm, k4[s]) * scale for s in range(n_src)]
            mx = functools.reduce(jnp.maximum, [jnp.max(t, axis=-1, keepdims=True) for t in sc])
            p = [jnp.exp(t - mx) for t in sc]
            den = functools.reduce(jnp.add, [jnp.sum(t, axis=-1, keepdims=True) for t in p])
            pv = functools.reduce(jnp.add, [
                _dot(p[s].astype(BF16), jnp.where(own, v4[s], jnp.zeros_like(v4[s])))
                for s in range(n_src)])
            acc = pv / den if acc is None else acc + pv / den
        o_ref[:, j * A_GRP * A_HD:(j + 1) * A_GRP * A_HD] = acc.astype(o_ref.dtype)


def _attn(q, sources, seq_len, tq):
    rows = q.shape[0]
    n_batch = rows // seq_len
    per = seq_len // tq
    in_specs = [pl.BlockSpec((tq, A_Q), lambda b, t: (b * per + t, 0)),
                _resident((A_KV, A_KVW, A_GRP * A_HD), lambda b, t: (0, 0, 0))]
    args = [q, jnp.asarray(_head_place_matrices(), BF16)]
    for k, v, lk in sources:
        in_specs += [pl.BlockSpec((lk, A_KVW), lambda b, t: (b, 0))] * 2
        args += [k, v]
    return pl.pallas_call(
        functools.partial(_attn_kernel, n_src=len(sources)),
        grid=(n_batch, per),
        in_specs=in_specs,
        out_specs=pl.BlockSpec((tq, A_Q), lambda b, t: (b * per + t, 0)),
        out_shape=jax.ShapeDtypeStruct((rows, A_Q), BF16),
        compiler_params=_cparams(2),
        name="attn",
    )(*args)


def _regroup_heads(a):
    lead = a.shape[:-1]
    a = a.reshape(lead + (2, M_GROUPS, M_GROUP_HEADS))
    a = jnp.swapaxes(a, -3, -2).reshape(lead + (M_GROUPS, 2 * M_GROUP_HEADS))
    pad = [(0, 0)] * (len(lead) + 1) + [(0, LANES - 2 * M_GROUP_HEADS)]
    return jnp.pad(a, pad).reshape(lead + (M_GROUPS * LANES,))


def kernel(x_prompt, x_sample, state_ssm, cache_k, cache_v, c, c_ctx,
           ln_g, ada_w, ada_b, ff1_w_in, ff1_w_out, ff2_w_in, ff2_w_out,
           m_w_in, m_conv_w, m_conv_b, m_dt_bias, m_a_log, m_d, m_norm_g, m_w_out,
           g_w_in, g_b_in, g_norm_g, g_w_s, g_b_s, g_w_out,
           f_w_out, f_b_out,
           a_w_qkv, a_q_norm, a_k_norm, a_w_o):
    xs = [x_prompt.reshape(TP, D_MODEL), x_sample.reshape(TS, D_MODEL)]
    seq = (SEQ, DEC_SEQ)

    conds = jnp.concatenate(
        [c_ctx[None], c, jnp.zeros((COND_ROWS - 1 - DEC_BATCH, D_MODEL), F32)], axis=0)
    mods = _adaln(conds, ada_w, ada_b)

    ff_w = {(0, 0): (ff1_w_in[0].astype(BF16), ff1_w_out[0].astype(BF16))}
    n_zx = M_INNER + M_CONV_CH
    m_w_in_t = jnp.swapaxes(m_w_in, 1, 2)
    mixer_weights = ([(m_w_in_t, n_zx, D_MODEL), (m_w_out, M_INNER, D_MODEL)],
                     [(g_w_in, D_MODEL, 2 * G_INNER), (g_w_out, G_INNER, D_MODEL)],
                     [(f_w_out, D_MODEL, D_MODEL)],
                     [(a_w_qkv, D_MODEL, A_Q + 2 * A_KVW), (a_w_o, A_Q, D_MODEL)])

    new_ssm, new_k, new_v = [], [], []
    for i in range(DEPTH):
        kind, j = i % N_MIXERS, i // N_MIXERS
        later = [(i, 1)] + ([(i + 1, 0)] if i + 1 < DEPTH else [])
        convert = []
        for layer, second in later:
            pair = (ff2_w_in, ff2_w_out) if second else (ff1_w_in, ff1_w_out)
            convert += [(pair[0], layer, D_MODEL, 2 * FFN_HIDDEN),
                        (pair[1], layer, FFN_HIDDEN, D_MODEL)]
        convert += [(w, j, n_rows, n_cols) for w, n_rows, n_cols in mixer_weights[kind]]
        w_in, w_out = ff_w[(i, 0)]
        xs[0], cast = _ffn(xs[0], mods, ln_g, w_in, w_out, i, False, 0, convert=convert)
        for n, key in enumerate(later):
            ff_w[key] = (cast[2 * n], cast[2 * n + 1])
        mixer_w = cast[2 * len(later):]
        xs[1], _ = _ffn(xs[1], mods, ln_g, w_in, w_out, i, False, 1)
        mix = [None, None]
        if kind == 0:
            w_zx_t, w_out = mixer_w
            w_dt = _regroup_heads(m_w_in[j, :, n_zx:]).astype(BF16)
            dtb = _regroup_heads(m_dt_bias.reshape(-1, 1, 2 * M_HEADS))
            alog = _regroup_heads(m_a_log.reshape(-1, 1, 2 * M_HEADS))
            dsk = jnp.repeat(m_d, M_HEADDIM, axis=-1)[:, None, :]
            conv_b = m_conv_b[:, None, :]
            norm_g = m_norm_g[:, None, :]
            h0 = state_ssm[:, j].reshape(DEC_BATCH, 2, M_INNER, M_STATE)
            for p in range(2):
                z, xbc, dt = _modlinear(xs[p], mods, ln_g, [w_zx_t, w_dt], i, p,
                                        ((M_INNER, M_CONV_CH), (M_GROUPS * LANES,)),
                                        (True, False))
                res = _ssd(xbc, z, dt, m_conv_w, conv_b, dtb, alog, dsk, norm_g,
                           h0 if p == 1 else None, j, seq[p], emit_state=(p == 0))
                if p == 0:
                    new_ssm.append(res[1].reshape(BATCH, 2, M_HEADS, M_HEADDIM, M_STATE))
                mix[p] = (res[0], w_out)
        elif kind == 1:
            w_in, w_out = mixer_w
            w_s = g_w_s.astype(BF16)
            b_s_t = jnp.swapaxes(g_b_s, -1, -2)
            xs = [_gmlp(xs[p], mods, ln_g, w_in, g_b_in[:, None, :], g_norm_g[:, None, :],
                        w_s, b_s_t, w_out, i, j, p) for p in range(2)]
        elif kind == 2:
            w_out, = mixer_w
            xs = [_fnet(xs[p], mods, ln_g, w_out, f_b_out[:, None, :], i, j, p, seq[p])
                  for p in range(2)]
        else:
            w_qkv, w_o = mixer_w
            q_gain = jnp.tile(a_q_norm, (1, A_HEADS))[:, None, :]
            k_gain = jnp.tile(a_k_norm, (1, A_KV))[:, None, :]
            qp, kp, vp, kt, vt = _qkv(xs[0], mods, ln_g, w_qkv, q_gain, k_gain, i, j, 0)
            for cache, t in ((new_k, kt), (new_v, vt)):
                cache.append(jnp.transpose(t.reshape(BATCH, A_KV, A_HD, SEQ), (0, 3, 1, 2)))
            op = _attn(qp, [(kp, vp, SEQ)], SEQ, SEQ)
            ql, kl, vl = _qkv(xs[1], mods, ln_g, w_qkv, q_gain, k_gain, i, j, 1)
            kc = cache_k[:, j].reshape(DEC_BATCH * PAST_LEN, A_KVW)
            vc = cache_v[:, j].reshape(DEC_BATCH * PAST_LEN, A_KVW)
            ol = _attn(ql, [(kc, vc, PAST_LEN), (kl, vl, DEC_SEQ)], DEC_SEQ, 256)
            mix = [(op, w_o), (ol, w_o)]
        w_in, w_out = ff_w[(i, 1)]
        xs = [_ffn(xs[p], mods, ln_g, w_in, w_out, i, True, p, mix[p])[0] for p in range(2)]

    return (xs[0].reshape(BATCH, SEQ, D_MODEL),
            xs[1].reshape(DEC_BATCH, DEC_SEQ, D_MODEL),
            jnp.stack(new_ssm, axis=1),
            jnp.stack(new_k, axis=1),
            jnp.stack(new_v, axis=1))
```

```python
import functools
import math

import numpy as np
import jax
import jax.numpy as jnp
from jax import lax
from jax.experimental import pallas as pl
from jax.experimental.pallas import tpu as pltpu

F32 = jnp.float32
BF16 = jnp.bfloat16

D_MODEL = 1024
BATCH = 32
SEQ = 256
DEPTH = 4
DEC_BATCH = 2
DEC_SEQ = 1024
PAST_LEN = 256
GRID_W = 64
N_MIXERS = 4
N_MOD = 9
EPS = 1e-6
FFN_HIDDEN = 2816

M_INNER = 2 * D_MODEL
M_HEADDIM = 64
M_HEADS = M_INNER // M_HEADDIM
M_GROUPS = 4
M_STATE = 128
M_CONV = 3
M_CHUNK = 128
M_CONV_CH = M_INNER + 2 * M_GROUPS * M_STATE
M_GROUP_CH = M_INNER // M_GROUPS
M_GROUP_HEADS = M_HEADS // M_GROUPS

G_CHUNK = 128
G_INNER = 2 * D_MODEL
G_HEADS = 8
G_HEAD_CH = G_INNER // G_HEADS

F_GROUPS = 4
F_GROUP_CH = D_MODEL // F_GROUPS

A_HEADS = 16
A_KV = 4
A_HD = 64
A_GRP = A_HEADS // A_KV
A_Q = A_HEADS * A_HD
A_KVW = A_KV * A_HD
ROPE_THETA = 10000.0

LANES = 128
COND_ROWS = 8
TP = BATCH * SEQ
TS = DEC_BATCH * DEC_SEQ
VMEM_LIMIT = 56 * 2**20


def _cparams(n_grid):
    return pltpu.CompilerParams(dimension_semantics=("arbitrary",) * n_grid,
                                vmem_limit_bytes=VMEM_LIMIT)


def _resident(block_shape, index_map):
    return pl.BlockSpec(block_shape, index_map, pipeline_mode=pl.Buffered(1))


def _dot(a, b):
    return jnp.dot(a, b, preferred_element_type=F32)


def _dot_nt(a, b):
    return lax.dot_general(a, b, (((1,), (1,)), ((), ())), preferred_element_type=F32)


def _silu(x):
    return x * jax.nn.sigmoid(x)


def _softplus(x):
    return jnp.maximum(x, 0.0) + jnp.log1p(jnp.exp(-jnp.abs(x)))


def _modnorm(x, g, shift, scale):
    y = x * lax.rsqrt(jnp.mean(x * x, axis=-1, keepdims=True) + EPS) * g
    return y * (1.0 + scale) + shift


def _split3(v):
    hi = v.astype(BF16)
    r1 = v - hi.astype(F32)
    mid = r1.astype(BF16)
    lo = (r1 - mid.astype(F32)).astype(BF16)
    return hi, mid, lo


def _split2(v):
    hi = v.astype(BF16)
    return hi, (v - hi.astype(F32)).astype(BF16)


def _adaln_kernel(c_ref, w_ref, b_ref, o_ref):
    s = _silu(c_ref[...]).astype(BF16)
    o_ref[...] = _dot(s, w_ref[...].astype(BF16)) + b_ref[...]


def _adaln(conds, ada_w, ada_b):
    tn = 2304
    n = N_MOD * D_MODEL
    out = pl.pallas_call(
        _adaln_kernel,
        grid=(DEPTH, n // tn),
        in_specs=[pl.BlockSpec((COND_ROWS, D_MODEL), lambda i, j: (0, 0)),
                  pl.BlockSpec((None, D_MODEL, tn), lambda i, j: (i, 0, j)),
                  pl.BlockSpec((None, 1, tn), lambda i, j: (i, 0, j))],
        out_specs=pl.BlockSpec((None, COND_ROWS, tn), lambda i, j: (i, 0, j)),
        out_shape=jax.ShapeDtypeStruct((DEPTH, COND_ROWS, n), F32),
        compiler_params=_cparams(2),
        name="adaln",
    )(conds, ada_w, ada_b.reshape(DEPTH, 1, n))
    return out.reshape(DEPTH, COND_ROWS, N_MOD, D_MODEL)


def _mod_spec(layer, pop, tm):
    if pop == 0:
        return pl.BlockSpec((None, None, N_MOD, D_MODEL), lambda t, *_: (layer, 0, 0, 0))
    return pl.BlockSpec((None, None, N_MOD, D_MODEL),
                        lambda t, *_: (layer, 1 + (t * tm) // DEC_SEQ, 0, 0))


def _ln_spec(layer):
    return pl.BlockSpec((None, 3, D_MODEL), lambda t, *_: (layer, 0, 0))


MXU_COLS = 256
FFN_PIECES = tuple((lo, min(lo + 3 * MXU_COLS, FFN_HIDDEN))
                   for lo in range(0, FFN_HIDDEN, 3 * MXU_COLS))


def _ffn_kernel(*refs, k0, gk, has_mix, n_convert):
    it = iter(refs)
    if has_mix:
        y_ref, wmix_ref = next(it), next(it)
    x_ref, mod_ref, g_ref, win_hbm, wout_hbm = (next(it) for _ in range(5))
    src_refs = [next(it) for _ in range(n_convert)]
    o_ref = next(it)
    dst_refs = [next(it) for _ in range(n_convert)]
    win_ref, wout_ref, sem = (next(it) for _ in range(3))
    first = pl.program_id(0) == 0

    def piece_copies(p):
        lo, hi = FFN_PIECES[p]
        cols = pl.ds(lo, hi - lo)
        up_cols = pl.ds(FFN_HIDDEN + lo, hi - lo)
        return (pltpu.make_async_copy(win_hbm.at[:, cols], win_ref.at[:, cols], sem.at[p, 0]),
                pltpu.make_async_copy(win_hbm.at[:, up_cols], win_ref.at[:, up_cols], sem.at[p, 1]),
                pltpu.make_async_copy(wout_hbm.at[cols, :], wout_ref.at[cols, :], sem.at[p, 2]))

    @pl.when(first)
    def _():
        for p in range(len(FFN_PIECES)):
            for cp in piece_copies(p):
                cp.start()

    x = x_ref[...]
    if has_mix:
        x = x + mod_ref[5:6, :] * _dot(y_ref[...], wmix_ref[...])
    h = _modnorm(x, g_ref[gk:gk + 1, :], mod_ref[k0:k0 + 1, :], mod_ref[k0 + 1:k0 + 2, :])
    h = h.astype(BF16)

    def half_step(wait_for_weights):
        o = None
        for p, (lo, hi) in enumerate(FFN_PIECES):
            if wait_for_weights:
                for cp in piece_copies(p):
                    cp.wait()
            gate = _dot(h, win_ref[:, lo:hi])
            up = _dot(h, win_ref[:, FFN_HIDDEN + lo:FFN_HIDDEN + hi])
            part = _dot((_silu(gate) * up).astype(BF16), wout_ref[lo:hi, :])
            o = part if o is None else o + part
        o_ref[...] = x + 0.5 * mod_ref[k0 + 2:k0 + 3, :] * o

    pl.when(first)(lambda: half_step(True))
    pl.when(jnp.logical_not(first))(lambda: half_step(False))
    for src, dst in zip(src_refs, dst_refs):
        dst[...] = src[...].astype(BF16)


def _ffn(x, mods, ln_g, w_in, w_out, layer, second, pop, mix=None, convert=()):
    tm = 512
    rows = x.shape[0]
    steps = rows // tm
    in_specs = [pl.BlockSpec((tm, D_MODEL), lambda t: (t, 0)),
                _mod_spec(layer, pop, tm),
                _ln_spec(layer),
                pl.BlockSpec(memory_space=pl.ANY),
                pl.BlockSpec(memory_space=pl.ANY)]
    args = [x, mods, ln_g, w_in, w_out]
    if mix is not None:
        y, w_mix = mix
        k = y.shape[1]
        in_specs = [pl.BlockSpec((tm, k), lambda t: (t, 0)),
                    _resident((k, D_MODEL), lambda t: (0, 0))] + in_specs
        args = [y, w_mix] + args
    out_specs = [pl.BlockSpec((tm, D_MODEL), lambda t: (t, 0))]
    out_shape = [jax.ShapeDtypeStruct((rows, D_MODEL), F32)]
    for src, index, n_rows, n_cols in convert:
        slab = n_rows // steps
        in_specs.append(pl.BlockSpec((None, slab, n_cols), lambda t, index=index: (index, t, 0)))
        args.append(src)
        out_specs.append(pl.BlockSpec((slab, n_cols), lambda t: (t, 0)))
        out_shape.append(jax.ShapeDtypeStruct((n_rows, n_cols), BF16))
    res = pl.pallas_call(
        functools.partial(_ffn_kernel, k0=6 if second else 0, gk=2 if second else 0,
                          has_mix=mix is not None, n_convert=len(convert)),
        grid=(steps,),
        in_specs=in_specs,
        out_specs=out_specs,
        out_shape=out_shape,
        scratch_shapes=[pltpu.VMEM((D_MODEL, 2 * FFN_HIDDEN), BF16),
                        pltpu.VMEM((FFN_HIDDEN, D_MODEL), BF16),
                        pltpu.SemaphoreType.DMA((len(FFN_PIECES), 3))],
        compiler_params=_cparams(1),
        name="ffn",
    )(*args)
    return res[0], list(res[1:])


def _modlinear_kernel(x_ref, mod_ref, g_ref, *refs, widths, transposed):
    w_refs, o_refs = refs[:len(widths)], iter(refs[len(widths):])
    h = _modnorm(x_ref[...], g_ref[1:2, :], mod_ref[3:4, :], mod_ref[4:5, :]).astype(BF16)
    for w_ref, group, w_is_t in zip(w_refs, widths, transposed):
        off = 0
        for wd in group:
            if w_is_t:
                next(o_refs)[...] = _dot_nt(h, w_ref[off:off + wd, :])
            else:
                next(o_refs)[...] = _dot(h, w_ref[:, off:off + wd])
            off += wd


def _modlinear(x, mods, ln_g, weights, layer, pop, widths, transposed):
    tm = 512
    rows = x.shape[0]
    flat = [wd for group in widths for wd in group]
    return pl.pallas_call(
        functools.partial(_modlinear_kernel, widths=widths, transposed=transposed),
        grid=(rows // tm,),
        in_specs=[pl.BlockSpec((tm, D_MODEL), lambda t: (t, 0)),
                  _mod_spec(layer, pop, tm),
                  _ln_spec(layer)]
                 + [_resident(w.shape, lambda t: (0, 0)) for w in weights],
        out_specs=[pl.BlockSpec((tm, wd), lambda t: (t, 0)) for wd in flat],
        out_shape=[jax.ShapeDtypeStruct((rows, wd), F32) for wd in flat],
        compiler_params=_cparams(1),
        name="modlinear",
    )(x, mods, ln_g, *weights)


SSD_ROWS_PER_STEP = 2048


def _ssd_kernel(*refs, seq_len, has_h0, emit_state, gps):
    it = iter(refs)
    x_ref, b_ref, c_ref, z_ref, dt_ref = (next(it) for _ in range(5))
    cwx_ref, cwb_ref, cwc_ref, cbx_ref, cbb_ref, cbc_ref = (next(it) for _ in range(6))
    dtb_ref, alog_ref, dsk_ref, ng_ref = (next(it) for _ in range(4))
    h0_ref = next(it) if has_h0 else None
    y_ref = next(it)
    st_ref = next(it) if emit_state else None
    scratch = [next(it) for _ in range(9)]
    n_chunks = seq_len // M_CHUNK

    row = lax.broadcasted_iota(jnp.int32, (M_CHUNK, M_CHUNK), 0)
    col = lax.broadcasted_iota(jnp.int32, (M_CHUNK, M_CHUNK), 1)
    keeps = (col <= row, col >= row)
    tris = [jnp.where(k, 1.0, 0.0).astype(BF16) for k in keeps]

    def conv_silu(ref, w_ref, bias_ref):
        v = ref[...]
        t = lax.broadcasted_iota(jnp.int32, v.shape, 0)
        prev = jnp.where(t == 0, 0.0, pltpu.roll(v, 1, 0))
        nxt = jnp.where(t == seq_len - 1, 0.0, pltpu.roll(v, seq_len - 1, 0))
        w = w_ref[...]
        return _silu(prev * w[0:1, :] + v * w[1:2, :] + nxt * w[2:3, :] + bias_ref[...])

    def head_rows(v8):
        return jnp.concatenate(
            [jnp.broadcast_to(v8[r:r + 1, :], (M_HEADDIM, v8.shape[1]))
             for r in range(M_GROUP_HEADS)], axis=0)

    def make_group(q):
        ch = pl.ds(q * M_GROUP_CH, M_GROUP_CH)
        st = pl.ds(q * M_STATE, M_STATE)
        hd = pl.ds(q * LANES, LANES)
        xs_s, xt_s, b_s, c_s, acum_s, a8_s, dt8_s, yt_s, h_s = (s.at[q] for s in scratch)

        def prologue():
            xs = conv_silu(x_ref.at[:, ch], cwx_ref.at[:, ch], cbx_ref.at[:, ch])
            xs_s[...] = xs
            b_s[...] = conv_silu(b_ref.at[:, st], cwb_ref.at[:, st], cbb_ref.at[:, st]).astype(BF16)
            c_s[...] = conv_silu(c_ref.at[:, st], cwc_ref.at[:, st], cbc_ref.at[:, st]).astype(BF16)
            dt = _softplus(dt_ref[:, hd] + dtb_ref[:, hd])
            dta = dt * (-jnp.exp(alog_ref[:, hd]))
            for c in range(n_chunks):
                r = slice(c * M_CHUNK, (c + 1) * M_CHUNK)
                dt_t = dt[r, :].T
                by_time = _split3(dta[r, :])
                by_head = _split3(dta[r, :].T)
                for d in range(2):
                    hl = slice(M_GROUP_HEADS * d, M_GROUP_HEADS * (d + 1))
                    acum_s[d, r, :] = functools.reduce(
                        jnp.add, [_dot(tris[d], p) for p in by_time])
                    acum_t = functools.reduce(jnp.add, [_dot_nt(p, tris[d]) for p in by_head])
                    a8_s[d, c] = acum_t[hl, :]
                    dt8_s[d, c] = dt_t[hl, :]
                for k in range(M_GROUP_CH // LANES):
                    xt_s[c, k * LANES:(k + 1) * LANES, :] = xs[r, k * LANES:(k + 1) * LANES].T
            yt_s[...] = jnp.zeros(yt_s.shape, F32)
            if has_h0:
                h_s[...] = h0_ref[:, ch, :]
            else:
                h_s[...] = jnp.zeros(h_s.shape, F32)

        def chunk_step(c, d):
            start = c * M_CHUNK
            rows = pl.ds(start if isinstance(c, int) else pl.multiple_of(start, M_CHUNK), M_CHUNK)
            keep = keeps[d]
            end = M_CHUNK - 1 if d == 0 else 0
            acum = acum_s[d, rows, :]
            a8 = a8_s[d, c]
            dt8 = dt8_s[d, c]
            a_end8 = jnp.broadcast_to(a8[:, end:end + 1], a8.shape)
            w8 = jnp.exp(a_end8 - a8) * dt8
            bc = b_s[rows, :]
            cc = c_s[rows, :]
            cb = _dot_nt(cc, bc)
            h_in = h_s[d]
            h_s[d] = (h_in * head_rows(jnp.exp(a_end8))
                      + _dot((xt_s[c] * head_rows(w8)).astype(BF16), bc))
            yt_s[c] += _dot_nt(h_in.astype(BF16), cc) * head_rows(jnp.exp(a8))
            for r in range(M_GROUP_HEADS):
                ln = M_GROUP_HEADS * d + r
                hs = slice(r * M_HEADDIM, (r + 1) * M_HEADDIM)
                seg = jnp.broadcast_to(acum[:, ln:ln + 1], (M_CHUNK, M_CHUNK)) - a8[r:r + 1, :]
                decay = jnp.exp(jnp.where(keep, seg, -jnp.inf))
                m = (cb * decay).astype(BF16)
                x_dt = (xt_s[c, hs, :] * dt8[r:r + 1, :]).astype(BF16)
                yt_s[c, hs, :] += _dot_nt(x_dt, m)

        def epilogue():
            for c in range(n_chunks):
                r = slice(c * M_CHUNK, (c + 1) * M_CHUNK)
                for k in range(M_GROUP_CH // LANES):
                    cs = slice(k * LANES, (k + 1) * LANES)
                    skip = dsk_ref[:, pl.ds(q * M_GROUP_CH + k * LANES, LANES)]
                    xs_s[r, cs] = xs_s[r, cs] * skip + yt_s[c, cs, :].T
            y = xs_s[...] * _silu(z_ref[:, ch])
            y = y * lax.rsqrt(jnp.mean(y * y, axis=-1, keepdims=True) + EPS) * ng_ref[:, ch]
            y_ref[:, ch] = y.astype(BF16)
            if emit_state:
                st_ref[:, ch, :] = h_s[...]

        return prologue, chunk_step, epilogue

    groups = [make_group(q) for q in range(gps)]
    for prologue, _, _ in groups:
        prologue()

    def both_directions(i):
        for _, chunk_step, _ in groups:
            chunk_step(i, 0)
        for _, chunk_step, _ in groups:
            chunk_step(n_chunks - 1 - i, 1)

    if n_chunks <= 2:
        for i in range(n_chunks):
            both_directions(i)
    else:
        def body(i, carry):
            both_directions(i)
            return carry
        lax.fori_loop(0, n_chunks, body, 0)

    for _, _, epilogue in groups:
        epilogue()


def _ssd(xbc, z, dt, conv_w, conv_b, dtb, alog, dsk, norm_g, h0, j, seq_len, emit_state):
    rows = xbc.shape[0]
    n_seq = rows // seq_len
    gps = min(M_GROUPS, SSD_ROWS_PER_STEP // seq_len)
    ch, st, hd = gps * M_GROUP_CH, gps * M_STATE, gps * LANES
    gx = M_INNER // st
    gc = gx + M_GROUPS * M_STATE // st
    has_h0 = h0 is not None
    n_chunks = seq_len // M_CHUNK
    st_block = (None, 2, ch, M_STATE)

    in_specs = [pl.BlockSpec((seq_len, ch), lambda s, g: (s, g)),
                pl.BlockSpec((seq_len, st), lambda s, g: (s, gx + g)),
                pl.BlockSpec((seq_len, st), lambda s, g: (s, gc + g)),
                pl.BlockSpec((seq_len, ch), lambda s, g: (s, g)),
                pl.BlockSpec((seq_len, hd), lambda s, g: (s, g)),
                pl.BlockSpec((None, M_CONV, ch), lambda s, g: (j, 0, g)),
                pl.BlockSpec((None, M_CONV, st), lambda s, g: (j, 0, gx + g)),
                pl.BlockSpec((None, M_CONV, st), lambda s, g: (j, 0, gc + g)),
                pl.BlockSpec((None, 1, ch), lambda s, g: (j, 0, g)),
                pl.BlockSpec((None, 1, st), lambda s, g: (j, 0, gx + g)),
                pl.BlockSpec((None, 1, st), lambda s, g: (j, 0, gc + g)),
                pl.BlockSpec((None, 1, hd), lambda s, g: (j, 0, g)),
                pl.BlockSpec((None, 1, hd), lambda s, g: (j, 0, g)),
                pl.BlockSpec((None, 1, ch), lambda s, g: (j, 0, g)),
                pl.BlockSpec((None, 1, ch), lambda s, g: (j, 0, g))]
    args = [xbc, xbc, xbc, z, dt, conv_w, conv_w, conv_w, conv_b, conv_b, conv_b,
            dtb, alog, dsk, norm_g]
    if has_h0:
        in_specs.append(pl.BlockSpec(st_block, lambda s, g: (s, 0, g, 0)))
        args.append(h0)
    out_specs = [pl.BlockSpec((seq_len, ch), lambda s, g: (s, g))]
    out_shape = [jax.ShapeDtypeStruct((rows, M_INNER), BF16)]
    if emit_state:
        out_specs.append(pl.BlockSpec(st_block, lambda s, g: (s, 0, g, 0)))
        out_shape.append(jax.ShapeDtypeStruct((n_seq, 2, M_INNER, M_STATE), F32))
    scratch = [pltpu.VMEM((gps, seq_len, M_GROUP_CH), F32),
               pltpu.VMEM((gps, n_chunks, M_GROUP_CH, M_CHUNK), F32),
               pltpu.VMEM((gps, seq_len, M_STATE), BF16),
               pltpu.VMEM((gps, seq_len, M_STATE), BF16),
               pltpu.VMEM((gps, 2, seq_len, LANES), F32),
               pltpu.VMEM((gps, 2, n_chunks, M_GROUP_HEADS, M_CHUNK), F32),
               pltpu.VMEM((gps, 2, n_chunks, M_GROUP_HEADS, M_CHUNK), F32),
               pltpu.VMEM((gps, n_chunks, M_GROUP_CH, M_CHUNK), F32),
               pltpu.VMEM((gps, 2, M_GROUP_CH, M_STATE), F32)]
    return pl.pallas_call(
        functools.partial(_ssd_kernel, seq_len=seq_len, has_h0=has_h0, emit_state=emit_state,
                          gps=gps),
        grid=(n_seq, M_GROUPS // gps),
        in_specs=in_specs,
        out_specs=out_specs,
        out_shape=out_shape,
        scratch_shapes=scratch,
        compiler_params=_cparams(2),
        name="ssd",
    )(*args)


def _gmlp_kernel(x_ref, mod_ref, g_ref, win_ref, bin_ref, ng_ref, ws_ref, bs_ref, wout_ref,
                 o_ref, gate_s, *, tm):
    x = x_ref[...]
    h = _modnorm(x, g_ref[1:2, :], mod_ref[3:4, :], mod_ref[4:5, :]).astype(BF16)
    hg = jax.nn.gelu(_dot(h, win_ref[...]) + bin_ref[...])
    u = hg[:, :G_INNER]
    v = hg[:, G_INNER:]
    v = v * lax.rsqrt(jnp.mean(v * v, axis=-1, keepdims=True) + EPS) * ng_ref[...]
    vb = v.astype(BF16)
    bs = bs_ref[...]
    for c in range(tm // G_CHUNK):
        r = slice(c * G_CHUNK, (c + 1) * G_CHUNK)
        for hd in range(G_HEADS):
            cs = slice(hd * G_HEAD_CH, (hd + 1) * G_HEAD_CH)
            sv = _dot(ws_ref[hd], vb[r, cs]) + bs[:, hd:hd + 1]
            gate_s[r, cs] = (u[r, cs] * sv).astype(BF16)
    o_ref[...] = x + mod_ref[5:6, :] * _dot(gate_s[...], wout_ref[...])


def _gmlp(x, mods, ln_g, w_in, b_in, norm_g, w_s, b_s_t, w_out, layer, j, pop):
    tm = 512
    rows = x.shape[0]
    return pl.pallas_call(
        functools.partial(_gmlp_kernel, tm=tm),
        grid=(rows // tm,),
        in_specs=[pl.BlockSpec((tm, D_MODEL), lambda t: (t, 0)),
                  _mod_spec(layer, pop, tm),
                  _ln_spec(layer),
                  _resident((D_MODEL, 2 * G_INNER), lambda t: (0, 0)),
                  _resident((None, 1, 2 * G_INNER), lambda t: (j, 0, 0)),
                  _resident((None, 1, G_INNER), lambda t: (j, 0, 0)),
                  _resident((None, G_HEADS, G_CHUNK, G_CHUNK), lambda t: (j, 0, 0, 0)),
                  _resident((None, G_CHUNK, G_HEADS), lambda t: (j, 0, 0)),
                  _resident((G_INNER, D_MODEL), lambda t: (0, 0))],
        out_specs=pl.BlockSpec((tm, D_MODEL), lambda t: (t, 0)),
        out_shape=jax.ShapeDtypeStruct((rows, D_MODEL), F32),
        scratch_shapes=[pltpu.VMEM((tm, G_INNER), BF16)],
        compiler_params=_cparams(1),
        name="gmlp",
    )(x, mods, ln_g, w_in, b_in, norm_g, w_s, b_s_t, w_out)


def _dft_tables(seq_len):
    def angles(n):
        k = np.arange(n, dtype=np.int64)
        return 2.0 * np.pi * ((k[:, None] * k[None, :]) % n).astype(np.float64) / n
    ac = angles(F_GROUP_CH)
    al = angles(seq_len)
    chan = np.concatenate([np.cos(ac), np.sin(ac)], axis=1).astype(np.float32)
    pos = np.concatenate([np.cos(al), -np.sin(al)], axis=1).astype(np.float32)
    return chan, pos


def _fnet_kernel(x_ref, mod_ref, g_ref, chan_ref, pos_ref, wout_ref, b_ref, o_ref, f_s,
                 *, seq_len):
    x = x_ref[...]
    h = _modnorm(x, g_ref[1:2, :], mod_ref[3:4, :], mod_ref[4:5, :]).astype(BF16)
    chan = chan_ref[...].astype(BF16)
    pos = pos_ref[...].astype(BF16)
    scale = 1.0 / math.sqrt(seq_len * F_GROUP_CH)
    for g in range(F_GROUPS):
        cs = slice(g * F_GROUP_CH, (g + 1) * F_GROUP_CH)
        p = _dot(h[:, cs], chan)
        for s in range(x.shape[0] // seq_len):
            r = slice(s * seq_len, (s + 1) * seq_len)
            stacked = jnp.concatenate([p[r, :F_GROUP_CH], p[r, F_GROUP_CH:]], axis=0)
            f_s[r, cs] = (_dot(pos, stacked.astype(BF16)) * scale).astype(BF16)
    o_ref[...] = x + mod_ref[5:6, :] * (_dot(f_s[...], wout_ref[...]) + b_ref[...])


def _fnet(x, mods, ln_g, w_out, b_out, layer, j, pop, seq_len):
    rows = x.shape[0]
    chan, pos = _dft_tables(seq_len)
    tm = max(seq_len, 512)
    return pl.pallas_call(
        functools.partial(_fnet_kernel, seq_len=seq_len),
        grid=(rows // tm,),
        in_specs=[pl.BlockSpec((tm, D_MODEL), lambda t: (t, 0)),
                  _mod_spec(layer, pop, tm),
                  _ln_spec(layer),
                  _resident(chan.shape, lambda t: (0, 0)),
                  _resident(pos.shape, lambda t: (0, 0)),
                  _resident((D_MODEL, D_MODEL), lambda t: (0, 0)),
                  _resident((None, 1, D_MODEL), lambda t: (j, 0, 0))],
        out_specs=pl.BlockSpec((tm, D_MODEL), lambda t: (t, 0)),
        out_shape=jax.ShapeDtypeStruct((rows, D_MODEL), F32),
        scratch_shapes=[pltpu.VMEM((tm, D_MODEL), BF16)],
        compiler_params=_cparams(1),
        name="fnet",
    )(x, mods, ln_g, jnp.asarray(chan), jnp.asarray(pos), w_out, b_out)


def _head_mean_matrix():
    i = np.arange(2 * LANES)
    return ((i[:, None] // A_HD) == (i[None, :] // A_HD)).astype(np.float32) / A_HD


def _rope_tables():
    half = A_HD // 4
    inv = (np.float32(ROPE_THETA) ** (-np.arange(half, dtype=np.float32) / np.float32(half)))
    inv = inv.astype(np.float32)
    t = np.arange(DEC_SEQ)
    lane = np.arange(LANES)
    hl = lane % A_HD
    posn = np.where(hl[None, :] < A_HD // 2, (t // GRID_W)[:, None], (t % GRID_W)[:, None])
    sub = hl % (A_HD // 2)
    ang = posn.astype(np.float32) * inv[sub % half][None, :]
    cos = np.cos(ang).astype(np.float32)
    sin = np.sin(ang).astype(np.float32)
    lower = (sub < half)[None, :]
    sin_up = np.where(lower, -sin, 0.0).astype(np.float32)
    sin_dn = np.where(lower, 0.0, sin).astype(np.float32)
    return cos, sin_up, sin_dn


def _qkv_kernel(*refs, rope, cache_seq):
    it = iter(refs)
    x_ref, mod_ref, g_ref, w_ref, qg_ref, kg_ref, avg_ref = (next(it) for _ in range(7))
    if rope:
        cos_ref, sup_ref, sdn_ref = (next(it) for _ in range(3))
    q_ref, k_ref, v_ref = (next(it) for _ in range(3))
    if cache_seq:
        kt_ref, vt_ref = next(it), next(it)
    h = _modnorm(x_ref[...], g_ref[1:2, :], mod_ref[3:4, :], mod_ref[4:5, :]).astype(BF16)
    qkv = _dot(h, w_ref[...])
    avg = avg_ref[...]
    half = A_HD // 4

    def head_norm(t, gain):
        hi, lo = _split2(t * t)
        ms = _dot(hi, avg) + _dot(lo, avg)
        return t * lax.rsqrt(ms + EPS) * gain

    def rotate(t):
        up = pltpu.roll(t, LANES - half, 1)
        dn = pltpu.roll(t, half, 1)
        return t * cos_ref[...] + up * sup_ref[...] + dn * sdn_ref[...]

    slab = 2 * LANES
    for s in range((A_Q + A_KVW) // slab):
        cs = slice(s * slab, (s + 1) * slab)
        is_q = s < A_Q // slab
        gain = qg_ref[:, cs] if is_q else kg_ref[...]
        t = head_norm(qkv[:, cs], gain)
        if rope:
            t = jnp.concatenate([rotate(t[:, :LANES]), rotate(t[:, LANES:])], axis=1)
        if is_q:
            q_ref[:, cs] = t.astype(q_ref.dtype)
        else:
            k = t
    v = qkv[:, A_Q + A_KVW:]
    k_ref[...] = k.astype(k_ref.dtype)
    v_ref[...] = v.astype(v_ref.dtype)
    if cache_seq:
        for s in range(k.shape[0] // cache_seq):
            r = slice(s * cache_seq, (s + 1) * cache_seq)
            kt_ref[s] = k[r, :].T
            vt_ref[s] = v[r, :].T


def _qkv(x, mods, ln_g, w, q_gain, k_gain, layer, j, pop):
    tm = 512
    rows = x.shape[0]
    rope = pop == 1
    cache_seq = 0 if rope else SEQ
    in_specs = [pl.BlockSpec((tm, D_MODEL), lambda t: (t, 0)),
                _mod_spec(layer, pop, tm),
                _ln_spec(layer),
                _resident((D_MODEL, A_Q + 2 * A_KVW), lambda t: (0, 0)),
                _resident((None, 1, A_Q), lambda t: (j, 0, 0)),
                _resident((None, 1, A_KVW), lambda t: (j, 0, 0)),
                _resident((2 * LANES, 2 * LANES), lambda t: (0, 0))]
    args = [x, mods, ln_g, w, q_gain, k_gain, jnp.asarray(_head_mean_matrix(), BF16)]
    if rope:
        per_seq = DEC_SEQ // tm
        in_specs += [pl.BlockSpec((tm, LANES), lambda t: (t % per_seq, 0))] * 3
        args += [jnp.asarray(a) for a in _rope_tables()]
    out_specs = [pl.BlockSpec((tm, A_Q), lambda t: (t, 0)),
                 pl.BlockSpec((tm, A_KVW), lambda t: (t, 0)),
                 pl.BlockSpec((tm, A_KVW), lambda t: (t, 0))]
    out_shape = [jax.ShapeDtypeStruct((rows, A_Q), BF16),
                 jax.ShapeDtypeStruct((rows, A_KVW), BF16),
                 jax.ShapeDtypeStruct((rows, A_KVW), BF16)]
    if cache_seq:
        per_tile = tm // cache_seq
        out_specs += [pl.BlockSpec((per_tile, A_KVW, cache_seq), lambda t: (t, 0, 0))] * 2
        out_shape += [jax.ShapeDtypeStruct((rows // cache_seq, A_KVW, cache_seq), F32)] * 2
    return pl.pallas_call(
        functools.partial(_qkv_kernel, rope=rope, cache_seq=cache_seq),
        grid=(rows // tm,),
        in_specs=in_specs,
        out_specs=out_specs,
        out_shape=out_shape,
        compiler_params=_cparams(1),
        name="qkv",
    )(*args)


def _head_place_matrices():
    m = np.zeros((A_KV, A_KVW, A_GRP * A_HD), np.float32)
    d = np.arange(A_HD)
    for j in range(A_KV):
        for g in range(A_GRP):
            m[j, j * A_HD + d, g * A_HD + d] = 1.0
    return m


def _attn_kernel(q_ref, place_ref, *refs, n_src, n_batch):
    kv_refs, o_ref = refs[:2 * n_src], refs[2 * n_src]
    lane = lax.broadcasted_iota(jnp.int32, (1, A_GRP * A_HD), 1)
    scale = A_HD ** -0.5
    tq = q_ref.shape[0] // n_batch
    for j in range(A_KV):
        place = place_ref[j]
        cs = slice(j * A_GRP * A_HD, (j + 1) * A_GRP * A_HD)
        for b in range(n_batch):
            rq = slice(b * tq, (b + 1) * tq)
            qs = q_ref[rq, cs]
            keys = []
            for s in range(n_src):
                lk = kv_refs[2 * s].shape[0] // n_batch
                keys.append(slice(b * lk, (b + 1) * lk))
            k4 = [_dot(kv_refs[2 * s][keys[s], :].astype(BF16), place).astype(BF16)
                  for s in range(n_src)]
            v4 = [_dot(kv_refs[2 * s + 1][keys[s], :].astype(BF16), place).astype(BF16)
                  for s in range(n_src)]
            acc = None
            for g in range(A_GRP):
                own = (lane >= g * A_HD) & (lane < (g + 1) * A_HD)
                qm = jnp.where(own, qs, jnp.zeros_like(qs))
                sc = [_dot_nt(qm, k4[s]) * scale for s in range(n_src)]
                mx = functools.reduce(jnp.maximum,
                                      [jnp.max(t, axis=-1, keepdims=True) for t in sc])
                p = [jnp.exp(t - mx) for t in sc]
                den = functools.reduce(jnp.add, [jnp.sum(t, axis=-1, keepdims=True) for t in p])
                pv = functools.reduce(jnp.add, [
                    _dot(p[s].astype(BF16), jnp.where(own, v4[s], jnp.zeros_like(v4[s])))
                    for s in range(n_src)])
                acc = pv / den if acc is None else acc + pv / den
            o_ref[rq, cs] = acc.astype(o_ref.dtype)


def _attn(q, sources, seq_len, tq, batches_per_step=1):
    rows = q.shape[0]
    nb = batches_per_step
    per = seq_len // tq
    assert nb == 1 or per == 1
    in_specs = [pl.BlockSpec((nb * tq, A_Q), lambda b, t: (b * per + t, 0)),
                _resident((A_KV, A_KVW, A_GRP * A_HD), lambda b, t: (0, 0, 0))]
    args = [q, jnp.asarray(_head_place_matrices(), BF16)]
    for k, v, lk in sources:
        in_specs += [pl.BlockSpec((nb * lk, A_KVW), lambda b, t: (b, 0))] * 2
        args += [k, v]
    return pl.pallas_call(
        functools.partial(_attn_kernel, n_src=len(sources), n_batch=nb),
        grid=(rows // (seq_len * nb), per),
        in_specs=in_specs,
        out_specs=pl.BlockSpec((nb * tq, A_Q), lambda b, t: (b * per + t, 0)),
        out_shape=jax.ShapeDtypeStruct((rows, A_Q), BF16),
        compiler_params=_cparams(2),
        name="attn",
    )(*args)


def _regroup_heads(a):
    lead = a.shape[:-1]
    a = a.reshape(lead + (2, M_GROUPS, M_GROUP_HEADS))
    a = jnp.swapaxes(a, -3, -2).reshape(lead + (M_GROUPS, 2 * M_GROUP_HEADS))
    pad = [(0, 0)] * (len(lead) + 1) + [(0, LANES - 2 * M_GROUP_HEADS)]
    return jnp.pad(a, pad).reshape(lead + (M_GROUPS * LANES,))


def kernel(x_prompt, x_sample, state_ssm, cache_k, cache_v, c, c_ctx,
           ln_g, ada_w, ada_b, ff1_w_in, ff1_w_out, ff2_w_in, ff2_w_out,
           m_w_in, m_conv_w, m_conv_b, m_dt_bias, m_a_log, m_d, m_norm_g, m_w_out,
           g_w_in, g_b_in, g_norm_g, g_w_s, g_b_s, g_w_out,
           f_w_out, f_b_out,
           a_w_qkv, a_q_norm, a_k_norm, a_w_o):
    xs = [x_prompt.reshape(TP, D_MODEL), x_sample.reshape(TS, D_MODEL)]
    seq = (SEQ, DEC_SEQ)

    conds = jnp.concatenate(
        [c_ctx[None], c, jnp.zeros((COND_ROWS - 1 - DEC_BATCH, D_MODEL), F32)], axis=0)
    mods = _adaln(conds, ada_w, ada_b)

    ff_w = {(0, 0): (ff1_w_in[0].astype(BF16), ff1_w_out[0].astype(BF16))}
    n_zx = M_INNER + M_CONV_CH
    m_w_in_t = jnp.swapaxes(m_w_in, 1, 2)
    mixer_weights = ([(m_w_in_t, n_zx, D_MODEL), (m_w_out, M_INNER, D_MODEL)],
                     [(g_w_in, D_MODEL, 2 * G_INNER), (g_w_out, G_INNER, D_MODEL)],
                     [(f_w_out, D_MODEL, D_MODEL)],
                     [(a_w_qkv, D_MODEL, A_Q + 2 * A_KVW), (a_w_o, A_Q, D_MODEL)])

    new_ssm, new_k, new_v = [], [], []
    for i in range(DEPTH):
        kind, j = i % N_MIXERS, i // N_MIXERS
        later = [(i, 1)] + ([(i + 1, 0)] if i + 1 < DEPTH else [])
        convert = []
        for layer, second in later:
            pair = (ff2_w_in, ff2_w_out) if second else (ff1_w_in, ff1_w_out)
            convert += [(pair[0], layer, D_MODEL, 2 * FFN_HIDDEN),
                        (pair[1], layer, FFN_HIDDEN, D_MODEL)]
        convert += [(w, j, n_rows, n_cols) for w, n_rows, n_cols in mixer_weights[kind]]
        w_in, w_out = ff_w[(i, 0)]
        xs[0], cast = _ffn(xs[0], mods, ln_g, w_in, w_out, i, False, 0, convert=convert)
        for n, key in enumerate(later):
            ff_w[key] = (cast[2 * n], cast[2 * n + 1])
        mixer_w = cast[2 * len(later):]
        xs[1], _ = _ffn(xs[1], mods, ln_g, w_in, w_out, i, False, 1)
        mix = [None, None]
        if kind == 0:
            w_zx_t, w_out = mixer_w
            w_dt = _regroup_heads(m_w_in[j, :, n_zx:]).astype(BF16)
            dtb = _regroup_heads(m_dt_bias.reshape(-1, 1, 2 * M_HEADS))
            alog = _regroup_heads(m_a_log.reshape(-1, 1, 2 * M_HEADS))
            dsk = jnp.repeat(m_d, M_HEADDIM, axis=-1)[:, None, :]
            conv_b = m_conv_b[:, None, :]
            norm_g = m_norm_g[:, None, :]
            h0 = state_ssm[:, j].reshape(DEC_BATCH, 2, M_INNER, M_STATE)
            for p in range(2):
                z, xbc, dt = _modlinear(xs[p], mods, ln_g, [w_zx_t, w_dt], i, p,
                                        ((M_INNER, M_CONV_CH), (M_GROUPS * LANES,)),
                                        (True, False))
                res = _ssd(xbc, z, dt, m_conv_w, conv_b, dtb, alog, dsk, norm_g,
                           h0 if p == 1 else None, j, seq[p], emit_state=(p == 0))
                if p == 0:
                    new_ssm.append(res[1].reshape(BATCH, 2, M_HEADS, M_HEADDIM, M_STATE))
                mix[p] = (res[0], w_out)
        elif kind == 1:
            w_in, w_out = mixer_w
            w_s = g_w_s.astype(BF16)
            b_s_t = jnp.swapaxes(g_b_s, -1, -2)
            xs = [_gmlp(xs[p], mods, ln_g, w_in, g_b_in[:, None, :], g_norm_g[:, None, :],
                        w_s, b_s_t, w_out, i, j, p) for p in range(2)]
        elif kind == 2:
            w_out, = mixer_w
            xs = [_fnet(xs[p], mods, ln_g, w_out, f_b_out[:, None, :], i, j, p, seq[p])
                  for p in range(2)]
        else:
            w_qkv, w_o = mixer_w
            q_gain = jnp.tile(a_q_norm, (1, A_HEADS))[:, None, :]
            k_gain = jnp.tile(a_k_norm, (1, A_KV))[:, None, :]
            qp, kp, vp, kt, vt = _qkv(xs[0], mods, ln_g, w_qkv, q_gain, k_gain, i, j, 0)
            for cache, t in ((new_k, kt), (new_v, vt)):
                cache.append(jnp.transpose(t.reshape(BATCH, A_KV, A_HD, SEQ), (0, 3, 1, 2)))
            op = _attn(qp, [(kp, vp, SEQ)], SEQ, SEQ, batches_per_step=2)
            ql, kl, vl = _qkv(xs[1], mods, ln_g, w_qkv, q_gain, k_gain, i, j, 1)
            kc = cache_k[:, j].reshape(DEC_BATCH * PAST_LEN, A_KVW)
            vc = cache_v[:, j].reshape(DEC_BATCH * PAST_LEN, A_KVW)
            ol = _attn(ql, [(kc, vc, PAST_LEN), (kl, vl, DEC_SEQ)], DEC_SEQ, 256)
            mix = [(op, w_o), (ol, w_o)]
        w_in, w_out = ff_w[(i, 1)]
        xs = [_ffn(xs[p], mods, ln_g, w_in, w_out, i, True, p, mix[p])[0] for p in range(2)]

    return (xs[0].reshape(BATCH, SEQ, D_MODEL),
            xs[1].reshape(DEC_BATCH, DEC_SEQ, D_MODEL),
            jnp.stack(new_ssm, axis=1),
            jnp.stack(new_k, axis=1),
            jnp.stack(new_v, axis=1))
```

```python
import functools
import math

import numpy as np
import jax
import jax.numpy as jnp
from jax import lax
from jax.experimental import pallas as pl
from jax.experimental.pallas import tpu as pltpu

F32 = jnp.float32
BF16 = jnp.bfloat16

D_MODEL = 1024
BATCH = 32
SEQ = 256
DEPTH = 4
DEC_BATCH = 2
DEC_SEQ = 1024
PAST_LEN = 256
GRID_W = 64
N_MIXERS = 4
N_MOD = 9
EPS = 1e-6
FFN_HIDDEN = 2816

M_INNER = 2 * D_MODEL
M_HEADDIM = 64
M_HEADS = M_INNER // M_HEADDIM
M_GROUPS = 4
M_STATE = 128
M_CONV = 3
M_CHUNK = 128
M_CONV_CH = M_INNER + 2 * M_GROUPS * M_STATE
M_GROUP_CH = M_INNER // M_GROUPS
M_GROUP_HEADS = M_HEADS // M_GROUPS

G_CHUNK = 128
G_INNER = 2 * D_MODEL
G_HEADS = 8
G_HEAD_CH = G_INNER // G_HEADS

F_GROUPS = 4
F_GROUP_CH = D_MODEL // F_GROUPS

A_HEADS = 16
A_KV = 4
A_HD = 64
A_GRP = A_HEADS // A_KV
A_Q = A_HEADS * A_HD
A_KVW = A_KV * A_HD
ROPE_THETA = 10000.0

LANES = 128
COND_ROWS = 8
TP = BATCH * SEQ
TS = DEC_BATCH * DEC_SEQ
VMEM_LIMIT = 56 * 2**20


def _cparams(n_grid):
    return pltpu.CompilerParams(dimension_semantics=("arbitrary",) * n_grid,
                                vmem_limit_bytes=VMEM_LIMIT)


def _resident(block_shape, index_map):
    return pl.BlockSpec(block_shape, index_map, pipeline_mode=pl.Buffered(1))


def _dot(a, b):
    return jnp.dot(a, b, preferred_element_type=F32)


def _dot_nt(a, b):
    return lax.dot_general(a, b, (((1,), (1,)), ((), ())), preferred_element_type=F32)


def _silu(x):
    return x * jax.nn.sigmoid(x)


def _softplus(x):
    return jnp.maximum(x, 0.0) + jnp.log1p(jnp.exp(-jnp.abs(x)))


def _modnorm(x, g, shift, scale):
    y = x * lax.rsqrt(jnp.mean(x * x, axis=-1, keepdims=True) + EPS) * g
    return y * (1.0 + scale) + shift


def _split3(v):
    hi = v.astype(BF16)
    r1 = v - hi.astype(F32)
    mid = r1.astype(BF16)
    lo = (r1 - mid.astype(F32)).astype(BF16)
    return hi, mid, lo


def _split2(v):
    hi = v.astype(BF16)
    return hi, (v - hi.astype(F32)).astype(BF16)


def _adaln_kernel(c_ref, w_ref, b_ref, o_ref):
    s = _silu(c_ref[...]).astype(BF16)
    o_ref[...] = _dot(s, w_ref[...].astype(BF16)) + b_ref[...]


def _adaln(conds, ada_w, ada_b):
    tn = 2304
    n = N_MOD * D_MODEL
    out = pl.pallas_call(
        _adaln_kernel,
        grid=(DEPTH, n // tn),
        in_specs=[pl.BlockSpec((COND_ROWS, D_MODEL), lambda i, j: (0, 0)),
                  pl.BlockSpec((None, D_MODEL, tn), lambda i, j: (i, 0, j)),
                  pl.BlockSpec((None, 1, tn), lambda i, j: (i, 0, j))],
        out_specs=pl.BlockSpec((None, COND_ROWS, tn), lambda i, j: (i, 0, j)),
        out_shape=jax.ShapeDtypeStruct((DEPTH, COND_ROWS, n), F32),
        compiler_params=_cparams(2),
        name="adaln",
    )(conds, ada_w, ada_b.reshape(DEPTH, 1, n))
    return out.reshape(DEPTH, COND_ROWS, N_MOD, D_MODEL)


def _mod_spec(layer, pop, tm):
    if pop == 0:
        return pl.BlockSpec((None, None, N_MOD, D_MODEL), lambda t, *_: (layer, 0, 0, 0))
    return pl.BlockSpec((None, None, N_MOD, D_MODEL),
                        lambda t, *_: (layer, 1 + (t * tm) // DEC_SEQ, 0, 0))


def _ln_spec(layer):
    return pl.BlockSpec((None, 3, D_MODEL), lambda t, *_: (layer, 0, 0))


MXU_COLS = 256
FFN_PIECES = tuple((lo, min(lo + 3 * MXU_COLS, FFN_HIDDEN))
                   for lo in range(0, FFN_HIDDEN, 3 * MXU_COLS))


def _ffn_kernel(*refs, k0, gk, has_mix, n_convert):
    it = iter(refs)
    if has_mix:
        y_ref, wmix_ref = next(it), next(it)
    x_ref, mod_ref, g_ref, win_hbm, wout_hbm = (next(it) for _ in range(5))
    src_refs = [next(it) for _ in range(n_convert)]
    o_ref = next(it)
    dst_refs = [next(it) for _ in range(n_convert)]
    win_ref, wout_ref, sem = (next(it) for _ in range(3))
    first = pl.program_id(0) == 0

    def piece_copies(p):
        lo, hi = FFN_PIECES[p]
        cols = pl.ds(lo, hi - lo)
        up_cols = pl.ds(FFN_HIDDEN + lo, hi - lo)
        return (pltpu.make_async_copy(win_hbm.at[:, cols], win_ref.at[:, cols], sem.at[p, 0]),
                pltpu.make_async_copy(win_hbm.at[:, up_cols], win_ref.at[:, up_cols], sem.at[p, 1]),
                pltpu.make_async_copy(wout_hbm.at[cols, :], wout_ref.at[cols, :], sem.at[p, 2]))

    @pl.when(first)
    def _():
        for p in range(len(FFN_PIECES)):
            for cp in piece_copies(p):
                cp.start()

    x = x_ref[...]
    if has_mix:
        x = x + mod_ref[5:6, :] * _dot(y_ref[...], wmix_ref[...])
    h = _modnorm(x, g_ref[gk:gk + 1, :], mod_ref[k0:k0 + 1, :], mod_ref[k0 + 1:k0 + 2, :])
    h = h.astype(BF16)

    def half_step(wait_for_weights):
        o = None
        for p, (lo, hi) in enumerate(FFN_PIECES):
            if wait_for_weights:
                for cp in piece_copies(p):
                    cp.wait()
            gate = _dot(h, win_ref[:, lo:hi])
            up = _dot(h, win_ref[:, FFN_HIDDEN + lo:FFN_HIDDEN + hi])
            part = _dot((_silu(gate) * up).astype(BF16), wout_ref[lo:hi, :])
            o = part if o is None else o + part
        o_ref[...] = x + 0.5 * mod_ref[k0 + 2:k0 + 3, :] * o

    pl.when(first)(lambda: half_step(True))
    pl.when(jnp.logical_not(first))(lambda: half_step(False))
    for src, dst in zip(src_refs, dst_refs):
        dst[...] = src[...].astype(BF16)


def _ffn(x, mods, ln_g, w_in, w_out, layer, second, pop, mix=None, convert=()):
    tm = 512
    rows = x.shape[0]
    steps = rows // tm
    in_specs = [pl.BlockSpec((tm, D_MODEL), lambda t: (t, 0)),
                _mod_spec(layer, pop, tm),
                _ln_spec(layer),
                pl.BlockSpec(memory_space=pl.ANY),
                pl.BlockSpec(memory_space=pl.ANY)]
    args = [x, mods, ln_g, w_in, w_out]
    if mix is not None:
        y, w_mix = mix
        k = y.shape[1]
        in_specs = [pl.BlockSpec((tm, k), lambda t: (t, 0)),
                    _resident((k, D_MODEL), lambda t: (0, 0))] + in_specs
        args = [y, w_mix] + args
    out_specs = [pl.BlockSpec((tm, D_MODEL), lambda t: (t, 0))]
    out_shape = [jax.ShapeDtypeStruct((rows, D_MODEL), F32)]
    for src, index, n_rows, n_cols in convert:
        slab = n_rows // steps
        in_specs.append(pl.BlockSpec((None, slab, n_cols), lambda t, index=index: (index, t, 0)))
        args.append(src)
        out_specs.append(pl.BlockSpec((slab, n_cols), lambda t: (t, 0)))
        out_shape.append(jax.ShapeDtypeStruct((n_rows, n_cols), BF16))
    res = pl.pallas_call(
        functools.partial(_ffn_kernel, k0=6 if second else 0, gk=2 if second else 0,
                          has_mix=mix is not None, n_convert=len(convert)),
        grid=(steps,),
        in_specs=in_specs,
        out_specs=out_specs,
        out_shape=out_shape,
        scratch_shapes=[pltpu.VMEM((D_MODEL, 2 * FFN_HIDDEN), BF16),
                        pltpu.VMEM((FFN_HIDDEN, D_MODEL), BF16),
                        pltpu.SemaphoreType.DMA((len(FFN_PIECES), 3))],
        compiler_params=_cparams(1),
        name="ffn",
    )(*args)
    return res[0], list(res[1:])


def _modlinear_kernel(x_ref, mod_ref, g_ref, *refs, widths, transposed):
    w_refs, o_refs = refs[:len(widths)], iter(refs[len(widths):])
    h = _modnorm(x_ref[...], g_ref[1:2, :], mod_ref[3:4, :], mod_ref[4:5, :]).astype(BF16)
    for w_ref, group, w_is_t in zip(w_refs, widths, transposed):
        off = 0
        for wd in group:
            if w_is_t:
                next(o_refs)[...] = _dot_nt(h, w_ref[off:off + wd, :])
            else:
                next(o_refs)[...] = _dot(h, w_ref[:, off:off + wd])
            off += wd


def _modlinear(x, mods, ln_g, weights, layer, pop, widths, transposed):
    tm = 512
    rows = x.shape[0]
    flat = [wd for group in widths for wd in group]
    return pl.pallas_call(
        functools.partial(_modlinear_kernel, widths=widths, transposed=transposed),
        grid=(rows // tm,),
        in_specs=[pl.BlockSpec((tm, D_MODEL), lambda t: (t, 0)),
                  _mod_spec(layer, pop, tm),
                  _ln_spec(layer)]
                 + [_resident(w.shape, lambda t: (0, 0)) for w in weights],
        out_specs=[pl.BlockSpec((tm, wd), lambda t: (t, 0)) for wd in flat],
        out_shape=[jax.ShapeDtypeStruct((rows, wd), F32) for wd in flat],
        compiler_params=_cparams(1),
        name="modlinear",
    )(x, mods, ln_g, *weights)


SSD_ROWS_PER_STEP = 2048


def _ssd_kernel(*refs, seq_len, has_h0, emit_state, gps):
    it = iter(refs)
    x_ref, b_ref, c_ref, z_ref, dt_ref = (next(it) for _ in range(5))
    cwx_ref, cwb_ref, cwc_ref, cbx_ref, cbb_ref, cbc_ref = (next(it) for _ in range(6))
    dtb_ref, alog_ref, dsk_ref, ng_ref = (next(it) for _ in range(4))
    h0_ref = next(it) if has_h0 else None
    y_ref = next(it)
    st_ref = next(it) if emit_state else None
    scratch = [next(it) for _ in range(9)]
    n_chunks = seq_len // M_CHUNK

    row = lax.broadcasted_iota(jnp.int32, (M_CHUNK, M_CHUNK), 0)
    col = lax.broadcasted_iota(jnp.int32, (M_CHUNK, M_CHUNK), 1)
    keeps = (col <= row, col >= row)
    tris = [jnp.where(k, 1.0, 0.0).astype(BF16) for k in keeps]

    def conv_silu(ref, w_ref, bias_ref):
        v = ref[...]
        t = lax.broadcasted_iota(jnp.int32, v.shape, 0)
        prev = jnp.where(t == 0, 0.0, pltpu.roll(v, 1, 0))
        nxt = jnp.where(t == seq_len - 1, 0.0, pltpu.roll(v, seq_len - 1, 0))
        w = w_ref[...]
        return _silu(prev * w[0:1, :] + v * w[1:2, :] + nxt * w[2:3, :] + bias_ref[...])

    def head_rows(v8):
        return jnp.concatenate(
            [jnp.broadcast_to(v8[r:r + 1, :], (M_HEADDIM, v8.shape[1]))
             for r in range(M_GROUP_HEADS)], axis=0)

    def make_group(q):
        ch = pl.ds(q * M_GROUP_CH, M_GROUP_CH)
        st = pl.ds(q * M_STATE, M_STATE)
        hd = pl.ds(q * LANES, LANES)
        xs_s, xt_s, b_s, c_s, acum_s, a8_s, dt8_s, yt_s, h_s = (s.at[q] for s in scratch)

        def prologue():
            xs = conv_silu(x_ref.at[:, ch], cwx_ref.at[:, ch], cbx_ref.at[:, ch])
            xs_s[...] = xs
            b_s[...] = conv_silu(b_ref.at[:, st], cwb_ref.at[:, st], cbb_ref.at[:, st]).astype(BF16)
            c_s[...] = conv_silu(c_ref.at[:, st], cwc_ref.at[:, st], cbc_ref.at[:, st]).astype(BF16)
            dt = _softplus(dt_ref[:, hd] + dtb_ref[:, hd])
            dta = dt * (-jnp.exp(alog_ref[:, hd]))
            for c in range(n_chunks):
                r = slice(c * M_CHUNK, (c + 1) * M_CHUNK)
                dt_t = dt[r, :].T
                by_time = _split3(dta[r, :])
                by_head = _split3(dta[r, :].T)
                for d in range(2):
                    hl = slice(M_GROUP_HEADS * d, M_GROUP_HEADS * (d + 1))
                    acum_s[d, r, :] = functools.reduce(
                        jnp.add, [_dot(tris[d], p) for p in by_time])
                    acum_t = functools.reduce(jnp.add, [_dot_nt(p, tris[d]) for p in by_head])
                    a8_s[d, c] = acum_t[hl, :]
                    dt8_s[d, c] = dt_t[hl, :]
                for k in range(M_GROUP_CH // LANES):
                    xt_s[c, k * LANES:(k + 1) * LANES, :] = xs[r, k * LANES:(k + 1) * LANES].T
            yt_s[...] = jnp.zeros(yt_s.shape, F32)
            if has_h0:
                h_s[...] = h0_ref[:, ch, :]
            else:
                h_s[...] = jnp.zeros(h_s.shape, F32)

        def chunk_step(c, d):
            start = c * M_CHUNK
            rows = pl.ds(start if isinstance(c, int) else pl.multiple_of(start, M_CHUNK), M_CHUNK)
            keep = keeps[d]
            end = M_CHUNK - 1 if d == 0 else 0
            acum = acum_s[d, rows, :]
            a8 = a8_s[d, c]
            dt8 = dt8_s[d, c]
            a_end8 = jnp.broadcast_to(a8[:, end:end + 1], a8.shape)
            w8 = jnp.exp(a_end8 - a8) * dt8
            bc = b_s[rows, :]
            cc = c_s[rows, :]
            cb = _dot_nt(cc, bc)
            h_in = h_s[d]
            h_s[d] = (h_in * head_rows(jnp.exp(a_end8))
                      + _dot((xt_s[c] * head_rows(w8)).astype(BF16), bc))
            yt_s[c] += _dot_nt(h_in.astype(BF16), cc) * head_rows(jnp.exp(a8))
            for r in range(M_GROUP_HEADS):
                ln = M_GROUP_HEADS * d + r
                hs = slice(r * M_HEADDIM, (r + 1) * M_HEADDIM)
                seg = jnp.broadcast_to(acum[:, ln:ln + 1], (M_CHUNK, M_CHUNK)) - a8[r:r + 1, :]
                decay = jnp.exp(jnp.where(keep, seg, -jnp.inf))
                m = (cb * decay).astype(BF16)
                x_dt = (xt_s[c, hs, :] * dt8[r:r + 1, :]).astype(BF16)
                yt_s[c, hs, :] += _dot_nt(x_dt, m)

        def epilogue():
            for c in range(n_chunks):
                r = slice(c * M_CHUNK, (c + 1) * M_CHUNK)
                for k in range(M_GROUP_CH // LANES):
                    cs = slice(k * LANES, (k + 1) * LANES)
                    skip = dsk_ref[:, pl.ds(q * M_GROUP_CH + k * LANES, LANES)]
                    xs_s[r, cs] = xs_s[r, cs] * skip + yt_s[c, cs, :].T
            y = xs_s[...] * _silu(z_ref[:, ch])
            y = y * lax.rsqrt(jnp.mean(y * y, axis=-1, keepdims=True) + EPS) * ng_ref[:, ch]
            y_ref[:, ch] = y.astype(BF16)
            if emit_state:
                st_ref[:, ch, :] = h_s[...]

        return prologue, chunk_step, epilogue

    groups = [make_group(q) for q in range(gps)]
    for prologue, _, _ in groups:
        prologue()

    def both_directions(i):
        for _, chunk_step, _ in groups:
            chunk_step(i, 0)
        for _, chunk_step, _ in groups:
            chunk_step(n_chunks - 1 - i, 1)

    if n_chunks <= 2:
        for i in range(n_chunks):
            both_directions(i)
    else:
        def body(i, carry):
            both_directions(i)
            return carry
        lax.fori_loop(0, n_chunks, body, 0)

    for _, _, epilogue in groups:
        epilogue()


def _ssd(xbc, z, dt, conv_w, conv_b, dtb, alog, dsk, norm_g, h0, j, seq_len, emit_state):
    rows = xbc.shape[0]
    n_seq = rows // seq_len
    gps = min(M_GROUPS, SSD_ROWS_PER_STEP // seq_len)
    ch, st, hd = gps * M_GROUP_CH, gps * M_STATE, gps * LANES
    gx = M_INNER // st
    gc = gx + M_GROUPS * M_STATE // st
    has_h0 = h0 is not None
    n_chunks = seq_len // M_CHUNK
    st_block = (None, 2, ch, M_STATE)

    in_specs = [pl.BlockSpec((seq_len, ch), lambda s, g: (s, g)),
                pl.BlockSpec((seq_len, st), lambda s, g: (s, gx + g)),
                pl.BlockSpec((seq_len, st), lambda s, g: (s, gc + g)),
                pl.BlockSpec((seq_len, ch), lambda s, g: (s, g)),
                pl.BlockSpec((seq_len, hd), lambda s, g: (s, g)),
                pl.BlockSpec((None, M_CONV, ch), lambda s, g: (j, 0, g)),
                pl.BlockSpec((None, M_CONV, st), lambda s, g: (j, 0, gx + g)),
                pl.BlockSpec((None, M_CONV, st), lambda s, g: (j, 0, gc + g)),
                pl.BlockSpec((None, 1, ch), lambda s, g: (j, 0, g)),
                pl.BlockSpec((None, 1, st), lambda s, g: (j, 0, gx + g)),
                pl.BlockSpec((None, 1, st), lambda s, g: (j, 0, gc + g)),
                pl.BlockSpec((None, 1, hd), lambda s, g: (j, 0, g)),
                pl.BlockSpec((None, 1, hd), lambda s, g: (j, 0, g)),
                pl.BlockSpec((None, 1, ch), lambda s, g: (j, 0, g)),
                pl.BlockSpec((None, 1, ch), lambda s, g: (j, 0, g))]
    args = [xbc, xbc, xbc, z, dt, conv_w, conv_w, conv_w, conv_b, conv_b, conv_b,
            dtb, alog, dsk, norm_g]
    if has_h0:
        in_specs.append(pl.BlockSpec(st_block, lambda s, g: (s, 0, g, 0)))
        args.append(h0)
    out_specs = [pl.BlockSpec((seq_len, ch), lambda s, g: (s, g))]
    out_shape = [jax.ShapeDtypeStruct((rows, M_INNER), BF16)]
    if emit_state:
        out_specs.append(pl.BlockSpec(st_block, lambda s, g: (s, 0, g, 0)))
        out_shape.append(jax.ShapeDtypeStruct((n_seq, 2, M_INNER, M_STATE), F32))
    scratch = [pltpu.VMEM((gps, seq_len, M_GROUP_CH), F32),
               pltpu.VMEM((gps, n_chunks, M_GROUP_CH, M_CHUNK), F32),
               pltpu.VMEM((gps, seq_len, M_STATE), BF16),
               pltpu.VMEM((gps, seq_len, M_STATE), BF16),
               pltpu.VMEM((gps, 2, seq_len, LANES), F32),
               pltpu.VMEM((gps, 2, n_chunks, M_GROUP_HEADS, M_CHUNK), F32),
               pltpu.VMEM((gps, 2, n_chunks, M_GROUP_HEADS, M_CHUNK), F32),
               pltpu.VMEM((gps, n_chunks, M_GROUP_CH, M_CHUNK), F32),
               pltpu.VMEM((gps, 2, M_GROUP_CH, M_STATE), F32)]
    return pl.pallas_call(
        functools.partial(_ssd_kernel, seq_len=seq_len, has_h0=has_h0, emit_state=emit_state,
                          gps=gps),
        grid=(n_seq, M_GROUPS // gps),
        in_specs=in_specs,
        out_specs=out_specs,
        out_shape=out_shape,
        scratch_shapes=scratch,
        compiler_params=_cparams(2),
        name="ssd",
    )(*args)


def _gmlp_kernel(x_ref, mod_ref, g_ref, win_ref, bin_ref, ng_ref, ws_ref, bs_ref, wout_ref,
                 o_ref, gate_s, *, tm):
    x = x_ref[...]
    h = _modnorm(x, g_ref[1:2, :], mod_ref[3:4, :], mod_ref[4:5, :]).astype(BF16)
    hg = jax.nn.gelu(_dot(h, win_ref[...]) + bin_ref[...])
    u = hg[:, :G_INNER]
    v = hg[:, G_INNER:]
    v = v * lax.rsqrt(jnp.mean(v * v, axis=-1, keepdims=True) + EPS) * ng_ref[...]
    vb = v.astype(BF16)
    bs = bs_ref[...]
    for c in range(tm // G_CHUNK):
        r = slice(c * G_CHUNK, (c + 1) * G_CHUNK)
        for hd in range(G_HEADS):
            cs = slice(hd * G_HEAD_CH, (hd + 1) * G_HEAD_CH)
            sv = _dot(ws_ref[hd], vb[r, cs]) + bs[:, hd:hd + 1]
            gate_s[r, cs] = (u[r, cs] * sv).astype(BF16)
    o_ref[...] = x + mod_ref[5:6, :] * _dot(gate_s[...], wout_ref[...])


def _gmlp(x, mods, ln_g, w_in, b_in, norm_g, w_s, b_s_t, w_out, layer, j, pop):
    tm = 512
    rows = x.shape[0]
    return pl.pallas_call(
        functools.partial(_gmlp_kernel, tm=tm),
        grid=(rows // tm,),
        in_specs=[pl.BlockSpec((tm, D_MODEL), lambda t: (t, 0)),
                  _mod_spec(layer, pop, tm),
                  _ln_spec(layer),
                  _resident((D_MODEL, 2 * G_INNER), lambda t: (0, 0)),
                  _resident((None, 1, 2 * G_INNER), lambda t: (j, 0, 0)),
                  _resident((None, 1, G_INNER), lambda t: (j, 0, 0)),
                  _resident((None, G_HEADS, G_CHUNK, G_CHUNK), lambda t: (j, 0, 0, 0)),
                  _resident((None, G_CHUNK, G_HEADS), lambda t: (j, 0, 0)),
                  _resident((G_INNER, D_MODEL), lambda t: (0, 0))],
        out_specs=pl.BlockSpec((tm, D_MODEL), lambda t: (t, 0)),
        out_shape=jax.ShapeDtypeStruct((rows, D_MODEL), F32),
        scratch_shapes=[pltpu.VMEM((tm, G_INNER), BF16)],
        compiler_params=_cparams(1),
        name="gmlp",
    )(x, mods, ln_g, w_in, b_in, norm_g, w_s, b_s_t, w_out)


def _dft_tables(seq_len):
    def angles(n):
        k = np.arange(n, dtype=np.int64)
        return 2.0 * np.pi * ((k[:, None] * k[None, :]) % n).astype(np.float64) / n
    ac = angles(F_GROUP_CH)
    al = angles(seq_len)
    chan = np.concatenate([np.cos(ac), np.sin(ac)], axis=1).astype(np.float32)
    pos = np.concatenate([np.cos(al), -np.sin(al)], axis=1).astype(np.float32)
    return chan, pos


def _fnet_kernel(x_ref, mod_ref, g_ref, chan_ref, pos_ref, wout_ref, b_ref, o_ref, f_s,
                 *, seq_len):
    x = x_ref[...]
    h = _modnorm(x, g_ref[1:2, :], mod_ref[3:4, :], mod_ref[4:5, :]).astype(BF16)
    chan = chan_ref[...].astype(BF16)
    pos = pos_ref[...].astype(BF16)
    scale = 1.0 / math.sqrt(seq_len * F_GROUP_CH)
    for g in range(F_GROUPS):
        cs = slice(g * F_GROUP_CH, (g + 1) * F_GROUP_CH)
        p = _dot(h[:, cs], chan)
        for s in range(x.shape[0] // seq_len):
            r = slice(s * seq_len, (s + 1) * seq_len)
            stacked = jnp.concatenate([p[r, :F_GROUP_CH], p[r, F_GROUP_CH:]], axis=0)
            f_s[r, cs] = (_dot(pos, stacked.astype(BF16)) * scale).astype(BF16)
    o_ref[...] = x + mod_ref[5:6, :] * (_dot(f_s[...], wout_ref[...]) + b_ref[...])


def _fnet(x, mods, ln_g, w_out, b_out, layer, j, pop, seq_len):
    rows = x.shape[0]
    chan, pos = _dft_tables(seq_len)
    tm = max(seq_len, 1024)
    return pl.pallas_call(
        functools.partial(_fnet_kernel, seq_len=seq_len),
        grid=(rows // tm,),
        in_specs=[pl.BlockSpec((tm, D_MODEL), lambda t: (t, 0)),
                  _mod_spec(layer, pop, tm),
                  _ln_spec(layer),
                  _resident(chan.shape, lambda t: (0, 0)),
                  _resident(pos.shape, lambda t: (0, 0)),
                  _resident((D_MODEL, D_MODEL), lambda t: (0, 0)),
                  _resident((None, 1, D_MODEL), lambda t: (j, 0, 0))],
        out_specs=pl.BlockSpec((tm, D_MODEL), lambda t: (t, 0)),
        out_shape=jax.ShapeDtypeStruct((rows, D_MODEL), F32),
        scratch_shapes=[pltpu.VMEM((tm, D_MODEL), BF16)],
        compiler_params=_cparams(1),
        name="fnet",
    )(x, mods, ln_g, jnp.asarray(chan), jnp.asarray(pos), w_out, b_out)


def _head_mean_matrix():
    i = np.arange(2 * LANES)
    return ((i[:, None] // A_HD) == (i[None, :] // A_HD)).astype(np.float32) / A_HD


def _rope_tables():
    half = A_HD // 4
    inv = (np.float32(ROPE_THETA) ** (-np.arange(half, dtype=np.float32) / np.float32(half)))
    inv = inv.astype(np.float32)
    t = np.arange(DEC_SEQ)
    lane = np.arange(LANES)
    hl = lane % A_HD
    posn = np.where(hl[None, :] < A_HD // 2, (t // GRID_W)[:, None], (t % GRID_W)[:, None])
    sub = hl % (A_HD // 2)
    ang = posn.astype(np.float32) * inv[sub % half][None, :]
    cos = np.cos(ang).astype(np.float32)
    sin = np.sin(ang).astype(np.float32)
    lower = (sub < half)[None, :]
    sin_up = np.where(lower, -sin, 0.0).astype(np.float32)
    sin_dn = np.where(lower, 0.0, sin).astype(np.float32)
    return cos, sin_up, sin_dn


def _qkv_kernel(*refs, rope, cache_seq):
    it = iter(refs)
    x_ref, mod_ref, g_ref, w_ref, qg_ref, kg_ref, avg_ref = (next(it) for _ in range(7))
    if rope:
        cos_ref, sup_ref, sdn_ref = (next(it) for _ in range(3))
    q_ref, k_ref, v_ref = (next(it) for _ in range(3))
    if cache_seq:
        kt_ref, vt_ref = next(it), next(it)
    h = _modnorm(x_ref[...], g_ref[1:2, :], mod_ref[3:4, :], mod_ref[4:5, :]).astype(BF16)
    qkv = _dot(h, w_ref[...])
    avg = avg_ref[...]
    half = A_HD // 4

    def head_norm(t, gain):
        hi, lo = _split2(t * t)
        ms = _dot(hi, avg) + _dot(lo, avg)
        return t * lax.rsqrt(ms + EPS) * gain

    def rotate(t):
        up = pltpu.roll(t, LANES - half, 1)
        dn = pltpu.roll(t, half, 1)
        return t * cos_ref[...] + up * sup_ref[...] + dn * sdn_ref[...]

    slab = 2 * LANES
    for s in range((A_Q + A_KVW) // slab):
        cs = slice(s * slab, (s + 1) * slab)
        is_q = s < A_Q // slab
        gain = qg_ref[:, cs] if is_q else kg_ref[...]
        t = head_norm(qkv[:, cs], gain)
        if rope:
            t = jnp.concatenate([rotate(t[:, :LANES]), rotate(t[:, LANES:])], axis=1)
        if is_q:
            q_ref[:, cs] = t.astype(q_ref.dtype)
        else:
            k = t
    v = qkv[:, A_Q + A_KVW:]
    k_ref[...] = k.astype(k_ref.dtype)
    v_ref[...] = v.astype(v_ref.dtype)
    if cache_seq:
        for s in range(k.shape[0] // cache_seq):
            r = slice(s * cache_seq, (s + 1) * cache_seq)
            kt_ref[s] = k[r, :].T
            vt_ref[s] = v[r, :].T


def _qkv(x, mods, ln_g, w, q_gain, k_gain, layer, j, pop):
    tm = 512
    rows = x.shape[0]
    rope = pop == 1
    cache_seq = 0 if rope else SEQ
    in_specs = [pl.BlockSpec((tm, D_MODEL), lambda t: (t, 0)),
                _mod_spec(layer, pop, tm),
                _ln_spec(layer),
                _resident((D_MODEL, A_Q + 2 * A_KVW), lambda t: (0, 0)),
                _resident((None, 1, A_Q), lambda t: (j, 0, 0)),
                _resident((None, 1, A_KVW), lambda t: (j, 0, 0)),
                _resident((2 * LANES, 2 * LANES), lambda t: (0, 0))]
    args = [x, mods, ln_g, w, q_gain, k_gain, jnp.asarray(_head_mean_matrix(), BF16)]
    if rope:
        per_seq = DEC_SEQ // tm
        in_specs += [pl.BlockSpec((tm, LANES), lambda t: (t % per_seq, 0))] * 3
        args += [jnp.asarray(a) for a in _rope_tables()]
    out_specs = [pl.BlockSpec((tm, A_Q), lambda t: (t, 0)),
                 pl.BlockSpec((tm, A_KVW), lambda t: (t, 0)),
                 pl.BlockSpec((tm, A_KVW), lambda t: (t, 0))]
    out_shape = [jax.ShapeDtypeStruct((rows, A_Q), BF16),
                 jax.ShapeDtypeStruct((rows, A_KVW), BF16),
                 jax.ShapeDtypeStruct((rows, A_KVW), BF16)]
    if cache_seq:
        per_tile = tm // cache_seq
        out_specs += [pl.BlockSpec((per_tile, A_KVW, cache_seq), lambda t: (t, 0, 0))] * 2
        out_shape += [jax.ShapeDtypeStruct((rows // cache_seq, A_KVW, cache_seq), F32)] * 2
    return pl.pallas_call(
        functools.partial(_qkv_kernel, rope=rope, cache_seq=cache_seq),
        grid=(rows // tm,),
        in_specs=in_specs,
        out_specs=out_specs,
        out_shape=out_shape,
        compiler_params=_cparams(1),
        name="qkv",
    )(*args)


def _head_place_matrices():
    m = np.zeros((A_KV, A_KVW, A_GRP * A_HD), np.float32)
    d = np.arange(A_HD)
    for j in range(A_KV):
        for g in range(A_GRP):
            m[j, j * A_HD + d, g * A_HD + d] = 1.0
    return m


def _attn_kernel(q_ref, place_ref, *refs, n_src, n_batch):
    kv_refs, o_ref = refs[:2 * n_src], refs[2 * n_src]
    lane = lax.broadcasted_iota(jnp.int32, (1, A_GRP * A_HD), 1)
    scale = A_HD ** -0.5
    tq = q_ref.shape[0] // n_batch
    for j in range(A_KV):
        place = place_ref[j]
        cs = slice(j * A_GRP * A_HD, (j + 1) * A_GRP * A_HD)
        for b in range(n_batch):
            rq = slice(b * tq, (b + 1) * tq)
            qs = q_ref[rq, cs]
            keys = []
            for s in range(n_src):
                lk = kv_refs[2 * s].shape[0] // n_batch
                keys.append(slice(b * lk, (b + 1) * lk))
            k4 = [_dot(kv_refs[2 * s][keys[s], :].astype(BF16), place).astype(BF16)
                  for s in range(n_src)]
            v4 = [_dot(kv_refs[2 * s + 1][keys[s], :].astype(BF16), place).astype(BF16)
                  for s in range(n_src)]
            acc = None
            for g in range(A_GRP):
                own = (lane >= g * A_HD) & (lane < (g + 1) * A_HD)
                qm = jnp.where(own, qs, jnp.zeros_like(qs))
                sc = [_dot_nt(qm, k4[s]) * scale for s in range(n_src)]
                mx = functools.reduce(jnp.maximum,
                                      [jnp.max(t, axis=-1, keepdims=True) for t in sc])
                p = [jnp.exp(t - mx) for t in sc]
                den = functools.reduce(jnp.add, [jnp.sum(t, axis=-1, keepdims=True) for t in p])
                pv = functools.reduce(jnp.add, [
                    _dot(p[s].astype(BF16), jnp.where(own, v4[s], jnp.zeros_like(v4[s])))
                    for s in range(n_src)])
                acc = pv / den if acc is None else acc + pv / den
            o_ref[rq, cs] = acc.astype(o_ref.dtype)


def _attn(q, sources, seq_len, tq, batches_per_step=1):
    rows = q.shape[0]
    nb = batches_per_step
    per = seq_len // tq
    assert nb == 1 or per == 1
    in_specs = [pl.BlockSpec((nb * tq, A_Q), lambda b, t: (b * per + t, 0)),
                _resident((A_KV, A_KVW, A_GRP * A_HD), lambda b, t: (0, 0, 0))]
    args = [q, jnp.asarray(_head_place_matrices(), BF16)]
    for k, v, lk in sources:
        in_specs += [pl.BlockSpec((nb * lk, A_KVW), lambda b, t: (b, 0))] * 2
        args += [k, v]
    return pl.pallas_call(
        functools.partial(_attn_kernel, n_src=len(sources), n_batch=nb),
        grid=(rows // (seq_len * nb), per),
        in_specs=in_specs,
        out_specs=pl.BlockSpec((nb * tq, A_Q), lambda b, t: (b * per + t, 0)),
        out_shape=jax.ShapeDtypeStruct((rows, A_Q), BF16),
        compiler_params=_cparams(2),
        name="attn",
    )(*args)


def _regroup_heads(a):
    lead = a.shape[:-1]
    a = a.reshape(lead + (2, M_GROUPS, M_GROUP_HEADS))
    a = jnp.swapaxes(a, -3, -2).reshape(lead + (M_GROUPS, 2 * M_GROUP_HEADS))
    pad = [(0, 0)] * (len(lead) + 1) + [(0, LANES - 2 * M_GROUP_HEADS)]
    return jnp.pad(a, pad).reshape(lead + (M_GROUPS * LANES,))


def kernel(x_prompt, x_sample, state_ssm, cache_k, cache_v, c, c_ctx,
           ln_g, ada_w, ada_b, ff1_w_in, ff1_w_out, ff2_w_in, ff2_w_out,
           m_w_in, m_conv_w, m_conv_b, m_dt_bias, m_a_log, m_d, m_norm_g, m_w_out,
           g_w_in, g_b_in, g_norm_g, g_w_s, g_b_s, g_w_out,
           f_w_out, f_b_out,
           a_w_qkv, a_q_norm, a_k_norm, a_w_o):
    xs = [x_prompt.reshape(TP, D_MODEL), x_sample.reshape(TS, D_MODEL)]
    seq = (SEQ, DEC_SEQ)

    conds = jnp.concatenate(
        [c_ctx[None], c, jnp.zeros((COND_ROWS - 1 - DEC_BATCH, D_MODEL), F32)], axis=0)
    mods = _adaln(conds, ada_w, ada_b)

    ff_w = {(0, 0): (ff1_w_in[0].astype(BF16), ff1_w_out[0].astype(BF16))}
    n_zx = M_INNER + M_CONV_CH
    m_w_in_t = jnp.swapaxes(m_w_in, 1, 2)
    mixer_weights = ([(m_w_in_t, n_zx, D_MODEL), (m_w_out, M_INNER, D_MODEL)],
                     [(g_w_in, D_MODEL, 2 * G_INNER), (g_w_out, G_INNER, D_MODEL)],
                     [(f_w_out, D_MODEL, D_MODEL)],
                     [(a_w_qkv, D_MODEL, A_Q + 2 * A_KVW), (a_w_o, A_Q, D_MODEL)])

    new_ssm, new_k, new_v = [], [], []
    for i in range(DEPTH):
        kind, j = i % N_MIXERS, i // N_MIXERS
        later = [(i, 1)] + ([(i + 1, 0)] if i + 1 < DEPTH else [])
        convert = []
        for layer, second in later:
            pair = (ff2_w_in, ff2_w_out) if second else (ff1_w_in, ff1_w_out)
            convert += [(pair[0], layer, D_MODEL, 2 * FFN_HIDDEN),
                        (pair[1], layer, FFN_HIDDEN, D_MODEL)]
        convert += [(w, j, n_rows, n_cols) for w, n_rows, n_cols in mixer_weights[kind]]
        w_in, w_out = ff_w[(i, 0)]
        xs[0], cast = _ffn(xs[0], mods, ln_g, w_in, w_out, i, False, 0, convert=convert)
        for n, key in enumerate(later):
            ff_w[key] = (cast[2 * n], cast[2 * n + 1])
        mixer_w = cast[2 * len(later):]
        xs[1], _ = _ffn(xs[1], mods, ln_g, w_in, w_out, i, False, 1)
        mix = [None, None]
        if kind == 0:
            w_zx_t, w_out = mixer_w
            w_dt = _regroup_heads(m_w_in[j, :, n_zx:]).astype(BF16)
            dtb = _regroup_heads(m_dt_bias.reshape(-1, 1, 2 * M_HEADS))
            alog = _regroup_heads(m_a_log.reshape(-1, 1, 2 * M_HEADS))
            dsk = jnp.repeat(m_d, M_HEADDIM, axis=-1)[:, None, :]
            conv_b = m_conv_b[:, None, :]
            norm_g = m_norm_g[:, None, :]
            h0 = state_ssm[:, j].reshape(DEC_BATCH, 2, M_INNER, M_STATE)
            for p in range(2):
                z, xbc, dt = _modlinear(xs[p], mods, ln_g, [w_zx_t, w_dt], i, p,
                                        ((M_INNER, M_CONV_CH), (M_GROUPS * LANES,)),
                                        (True, False))
                res = _ssd(xbc, z, dt, m_conv_w, conv_b, dtb, alog, dsk, norm_g,
                           h0 if p == 1 else None, j, seq[p], emit_state=(p == 0))
                if p == 0:
                    new_ssm.append(res[1].reshape(BATCH, 2, M_HEADS, M_HEADDIM, M_STATE))
                mix[p] = (res[0], w_out)
        elif kind == 1:
            w_in, w_out = mixer_w
            w_s = g_w_s.astype(BF16)
            b_s_t = jnp.swapaxes(g_b_s, -1, -2)
            xs = [_gmlp(xs[p], mods, ln_g, w_in, g_b_in[:, None, :], g_norm_g[:, None, :],
                        w_s, b_s_t, w_out, i, j, p) for p in range(2)]
        elif kind == 2:
            w_out, = mixer_w
            xs = [_fnet(xs[p], mods, ln_g, w_out, f_b_out[:, None, :], i, j, p, seq[p])
                  for p in range(2)]
        else:
            w_qkv, w_o = mixer_w
            q_gain = jnp.tile(a_q_norm, (1, A_HEADS))[:, None, :]
            k_gain = jnp.tile(a_k_norm, (1, A_KV))[:, None, :]
            qp, kp, vp, kt, vt = _qkv(xs[0], mods, ln_g, w_qkv, q_gain, k_gain, i, j, 0)
            for cache, t in ((new_k, kt), (new_v, vt)):
                cache.append(jnp.transpose(t.reshape(BATCH, A_KV, A_HD, SEQ), (0, 3, 1, 2)))
            op = _attn(qp, [(kp, vp, SEQ)], SEQ, SEQ, batches_per_step=4)
            ql, kl, vl = _qkv(xs[1], mods, ln_g, w_qkv, q_gain, k_gain, i, j, 1)
            kc = cache_k[:, j].reshape(DEC_BATCH * PAST_LEN, A_KVW)
            vc = cache_v[:, j].reshape(DEC_BATCH * PAST_LEN, A_KVW)
            ol = _attn(ql, [(kc, vc, PAST_LEN), (kl, vl, DEC_SEQ)], DEC_SEQ, 256)
            mix = [(op, w_o), (ol, w_o)]
        w_in, w_out = ff_w[(i, 1)]
        xs = [_ffn(xs[p], mods, ln_g, w_in, w_out, i, True, p, mix[p])[0] for p in range(2)]

    return (xs[0].reshape(BATCH, SEQ, D_MODEL),
            xs[1].reshape(DEC_BATCH, DEC_SEQ, D_MODEL),
            jnp.stack(new_ssm, axis=1),
            jnp.stack(new_k, axis=1),
            jnp.stack(new_v, axis=1))
```

```python
import functools
import math

import numpy as np
import jax
import jax.numpy as jnp
from jax import lax
from jax.experimental import pallas as pl
from jax.experimental.pallas import tpu as pltpu

F32 = jnp.float32
BF16 = jnp.bfloat16

D_MODEL = 1024
BATCH = 32
SEQ = 256
DEPTH = 4
DEC_BATCH = 2
DEC_SEQ = 1024
PAST_LEN = 256
GRID_W = 64
N_MIXERS = 4
N_MOD = 9
EPS = 1e-6
FFN_HIDDEN = 2816

M_INNER = 2 * D_MODEL
M_HEADDIM = 64
M_HEADS = M_INNER // M_HEADDIM
M_GROUPS = 4
M_STATE = 128
M_CONV = 3
M_CHUNK = 128
M_CONV_CH = M_INNER + 2 * M_GROUPS * M_STATE
M_GROUP_CH = M_INNER // M_GROUPS
M_GROUP_HEADS = M_HEADS // M_GROUPS

G_CHUNK = 128
G_INNER = 2 * D_MODEL
G_HEADS = 8
G_HEAD_CH = G_INNER // G_HEADS

F_GROUPS = 4
F_GROUP_CH = D_MODEL // F_GROUPS

A_HEADS = 16
A_KV = 4
A_HD = 64
A_GRP = A_HEADS // A_KV
A_Q = A_HEADS * A_HD
A_KVW = A_KV * A_HD
ROPE_THETA = 10000.0

LANES = 128
COND_ROWS = 8
TP = BATCH * SEQ
TS = DEC_BATCH * DEC_SEQ
VMEM_LIMIT = 56 * 2**20


def _cparams(n_grid):
    return pltpu.CompilerParams(dimension_semantics=("arbitrary",) * n_grid,
                                vmem_limit_bytes=VMEM_LIMIT)


def _resident(block_shape, index_map):
    return pl.BlockSpec(block_shape, index_map, pipeline_mode=pl.Buffered(1))


def _dot(a, b):
    return jnp.dot(a, b, preferred_element_type=F32)


def _dot_nt(a, b):
    return lax.dot_general(a, b, (((1,), (1,)), ((), ())), preferred_element_type=F32)


def _silu(x):
    return x * jax.nn.sigmoid(x)


def _softplus(x):
    return jnp.maximum(x, 0.0) + jnp.log1p(jnp.exp(-jnp.abs(x)))


def _modnorm(x, g, shift, scale):
    y = x * lax.rsqrt(jnp.mean(x * x, axis=-1, keepdims=True) + EPS) * g
    return y * (1.0 + scale) + shift


def _split3(v):
    hi = v.astype(BF16)
    r1 = v - hi.astype(F32)
    mid = r1.astype(BF16)
    lo = (r1 - mid.astype(F32)).astype(BF16)
    return hi, mid, lo


def _split2(v):
    hi = v.astype(BF16)
    return hi, (v - hi.astype(F32)).astype(BF16)


def _adaln_kernel(c_ref, w_ref, b_ref, o_ref):
    s = _silu(c_ref[...]).astype(BF16)
    o_ref[...] = _dot(s, w_ref[...].astype(BF16)) + b_ref[...]


def _adaln(conds, ada_w, ada_b):
    tn = 2304
    n = N_MOD * D_MODEL
    out = pl.pallas_call(
        _adaln_kernel,
        grid=(DEPTH, n // tn),
        in_specs=[pl.BlockSpec((COND_ROWS, D_MODEL), lambda i, j: (0, 0)),
                  pl.BlockSpec((None, D_MODEL, tn), lambda i, j: (i, 0, j)),
                  pl.BlockSpec((None, 1, tn), lambda i, j: (i, 0, j))],
        out_specs=pl.BlockSpec((None, COND_ROWS, tn), lambda i, j: (i, 0, j)),
        out_shape=jax.ShapeDtypeStruct((DEPTH, COND_ROWS, n), F32),
        compiler_params=_cparams(2),
        name="adaln",
    )(conds, ada_w, ada_b.reshape(DEPTH, 1, n))
    return out.reshape(DEPTH, COND_ROWS, N_MOD, D_MODEL)


def _mod_spec(layer, pop, tm):
    if pop == 0:
        return pl.BlockSpec((None, None, N_MOD, D_MODEL), lambda t, *_: (layer, 0, 0, 0))
    return pl.BlockSpec((None, None, N_MOD, D_MODEL),
                        lambda t, *_: (layer, 1 + (t * tm) // DEC_SEQ, 0, 0))


def _ln_spec(layer):
    return pl.BlockSpec((None, 3, D_MODEL), lambda t, *_: (layer, 0, 0))


MXU_COLS = 256
FFN_PIECES = tuple((lo, min(lo + 3 * MXU_COLS, FFN_HIDDEN))
                   for lo in range(0, FFN_HIDDEN, 3 * MXU_COLS))


def _ffn_kernel(*refs, k0, gk, has_mix, n_convert):
    it = iter(refs)
    if has_mix:
        y_ref, wmix_ref = next(it), next(it)
    x_ref, mod_ref, g_ref, win_hbm, wout_hbm = (next(it) for _ in range(5))
    src_refs = [next(it) for _ in range(n_convert)]
    o_ref = next(it)
    dst_refs = [next(it) for _ in range(n_convert)]
    win_ref, wout_ref, sem = (next(it) for _ in range(3))
    first = pl.program_id(0) == 0

    def piece_copies(p):
        lo, hi = FFN_PIECES[p]
        cols = pl.ds(lo, hi - lo)
        up_cols = pl.ds(FFN_HIDDEN + lo, hi - lo)
        return (pltpu.make_async_copy(win_hbm.at[:, cols], win_ref.at[:, cols], sem.at[p, 0]),
                pltpu.make_async_copy(win_hbm.at[:, up_cols], win_ref.at[:, up_cols], sem.at[p, 1]),
                pltpu.make_async_copy(wout_hbm.at[cols, :], wout_ref.at[cols, :], sem.at[p, 2]))

    @pl.when(first)
    def _():
        for cp in piece_copies(0):
            cp.start()

    x = x_ref[...]
    if has_mix:
        x = x + mod_ref[5:6, :] * _dot(y_ref[...], wmix_ref[...])
    h = _modnorm(x, g_ref[gk:gk + 1, :], mod_ref[k0:k0 + 1, :], mod_ref[k0 + 1:k0 + 2, :])
    h = h.astype(BF16)

    def half_step(wait_for_weights):
        o = None
        for p, (lo, hi) in enumerate(FFN_PIECES):
            if wait_for_weights:
                for cp in piece_copies(p):
                    cp.wait()
                if p + 1 < len(FFN_PIECES):
                    for cp in piece_copies(p + 1):
                        cp.start()
            gate = _dot(h, win_ref[:, lo:hi])
            up = _dot(h, win_ref[:, FFN_HIDDEN + lo:FFN_HIDDEN + hi])
            part = _dot((_silu(gate) * up).astype(BF16), wout_ref[lo:hi, :])
            o = part if o is None else o + part
        o_ref[...] = x + 0.5 * mod_ref[k0 + 2:k0 + 3, :] * o

    pl.when(first)(lambda: half_step(True))
    pl.when(jnp.logical_not(first))(lambda: half_step(False))
    for src, dst in zip(src_refs, dst_refs):
        dst[...] = src[...].astype(BF16)


def _ffn(x, mods, ln_g, w_in, w_out, layer, second, pop, mix=None, convert=()):
    tm = 512
    rows = x.shape[0]
    steps = rows // tm
    in_specs = [pl.BlockSpec((tm, D_MODEL), lambda t: (t, 0)),
                _mod_spec(layer, pop, tm),
                _ln_spec(layer),
                pl.BlockSpec(memory_space=pl.ANY),
                pl.BlockSpec(memory_space=pl.ANY)]
    args = [x, mods, ln_g, w_in, w_out]
    if mix is not None:
        y, w_mix = mix
        k = y.shape[1]
        in_specs = [pl.BlockSpec((tm, k), lambda t: (t, 0)),
                    _resident((k, D_MODEL), lambda t: (0, 0))] + in_specs
        args = [y, w_mix] + args
    out_specs = [pl.BlockSpec((tm, D_MODEL), lambda t: (t, 0))]
    out_shape = [jax.ShapeDtypeStruct((rows, D_MODEL), F32)]
    for src, index, n_rows, n_cols in convert:
        slab = n_rows // steps
        in_specs.append(pl.BlockSpec((None, slab, n_cols), lambda t, index=index: (index, t, 0)))
        args.append(src)
        out_specs.append(pl.BlockSpec((slab, n_cols), lambda t: (t, 0)))
        out_shape.append(jax.ShapeDtypeStruct((n_rows, n_cols), BF16))
    res = pl.pallas_call(
        functools.partial(_ffn_kernel, k0=6 if second else 0, gk=2 if second else 0,
                          has_mix=mix is not None, n_convert=len(convert)),
        grid=(steps,),
        in_specs=in_specs,
        out_specs=out_specs,
        out_shape=out_shape,
        scratch_shapes=[pltpu.VMEM((D_MODEL, 2 * FFN_HIDDEN), BF16),
                        pltpu.VMEM((FFN_HIDDEN, D_MODEL), BF16),
                        pltpu.SemaphoreType.DMA((len(FFN_PIECES), 3))],
        compiler_params=_cparams(1),
        name="ffn",
    )(*args)
    return res[0], list(res[1:])


def _modlinear_kernel(x_ref, mod_ref, g_ref, *refs, widths, transposed):
    w_refs, o_refs = refs[:len(widths)], iter(refs[len(widths):])
    h = _modnorm(x_ref[...], g_ref[1:2, :], mod_ref[3:4, :], mod_ref[4:5, :]).astype(BF16)
    for w_ref, group, w_is_t in zip(w_refs, widths, transposed):
        off = 0
        for wd in group:
            if w_is_t:
                next(o_refs)[...] = _dot_nt(h, w_ref[off:off + wd, :])
            else:
                next(o_refs)[...] = _dot(h, w_ref[:, off:off + wd])
            off += wd


def _modlinear(x, mods, ln_g, weights, layer, pop, widths, transposed):
    tm = 512
    rows = x.shape[0]
    flat = [wd for group in widths for wd in group]
    return pl.pallas_call(
        functools.partial(_modlinear_kernel, widths=widths, transposed=transposed),
        grid=(rows // tm,),
        in_specs=[pl.BlockSpec((tm, D_MODEL), lambda t: (t, 0)),
                  _mod_spec(layer, pop, tm),
                  _ln_spec(layer)]
                 + [_resident(w.shape, lambda t: (0, 0)) for w in weights],
        out_specs=[pl.BlockSpec((tm, wd), lambda t: (t, 0)) for wd in flat],
        out_shape=[jax.ShapeDtypeStruct((rows, wd), F32) for wd in flat],
        compiler_params=_cparams(1),
        name="modlinear",
    )(x, mods, ln_g, *weights)


SSD_ROWS_PER_STEP = 2048


def _ssd_kernel(*refs, seq_len, has_h0, emit_state, gps):
    it = iter(refs)
    x_ref, b_ref, c_ref, z_ref, dt_ref = (next(it) for _ in range(5))
    cwx_ref, cwb_ref, cwc_ref, cbx_ref, cbb_ref, cbc_ref = (next(it) for _ in range(6))
    dtb_ref, alog_ref, dsk_ref, ng_ref = (next(it) for _ in range(4))
    h0_ref = next(it) if has_h0 else None
    y_ref = next(it)
    st_ref = next(it) if emit_state else None
    scratch = [next(it) for _ in range(9)]
    n_chunks = seq_len // M_CHUNK

    row = lax.broadcasted_iota(jnp.int32, (M_CHUNK, M_CHUNK), 0)
    col = lax.broadcasted_iota(jnp.int32, (M_CHUNK, M_CHUNK), 1)
    keeps = (col <= row, col >= row)
    tris = [jnp.where(k, 1.0, 0.0).astype(BF16) for k in keeps]

    def conv_silu(ref, w_ref, bias_ref):
        v = ref[...]
        t = lax.broadcasted_iota(jnp.int32, v.shape, 0)
        prev = jnp.where(t == 0, 0.0, pltpu.roll(v, 1, 0))
        nxt = jnp.where(t == seq_len - 1, 0.0, pltpu.roll(v, seq_len - 1, 0))
        w = w_ref[...]
        return _silu(prev * w[0:1, :] + v * w[1:2, :] + nxt * w[2:3, :] + bias_ref[...])

    def head_rows(v8):
        return jnp.concatenate(
            [jnp.broadcast_to(v8[r:r + 1, :], (M_HEADDIM, v8.shape[1]))
             for r in range(M_GROUP_HEADS)], axis=0)

    def make_group(q):
        ch = pl.ds(q * M_GROUP_CH, M_GROUP_CH)
        st = pl.ds(q * M_STATE, M_STATE)
        hd = pl.ds(q * LANES, LANES)
        xs_s, xt_s, b_s, c_s, acum_s, a8_s, dt8_s, yt_s, h_s = (s.at[q] for s in scratch)

        def prologue():
            xs = conv_silu(x_ref.at[:, ch], cwx_ref.at[:, ch], cbx_ref.at[:, ch])
            xs_s[...] = xs
            b_s[...] = conv_silu(b_ref.at[:, st], cwb_ref.at[:, st], cbb_ref.at[:, st]).astype(BF16)
            c_s[...] = conv_silu(c_ref.at[:, st], cwc_ref.at[:, st], cbc_ref.at[:, st]).astype(BF16)
            dt = _softplus(dt_ref[:, hd] + dtb_ref[:, hd])
            dta = dt * (-jnp.exp(alog_ref[:, hd]))
            for c in range(n_chunks):
                r = slice(c * M_CHUNK, (c + 1) * M_CHUNK)
                dt_t = dt[r, :].T
                by_time = _split3(dta[r, :])
                by_head = _split3(dta[r, :].T)
                for d in range(2):
                    hl = slice(M_GROUP_HEADS * d, M_GROUP_HEADS * (d + 1))
                    acum_s[d, r, :] = functools.reduce(
                        jnp.add, [_dot(tris[d], p) for p in by_time])
                    acum_t = functools.reduce(jnp.add, [_dot_nt(p, tris[d]) for p in by_head])
                    a8_s[d, c] = acum_t[hl, :]
                    dt8_s[d, c] = dt_t[hl, :]
                for k in range(M_GROUP_CH // LANES):
                    xt_s[c, k * LANES:(k + 1) * LANES, :] = xs[r, k * LANES:(k + 1) * LANES].T
            yt_s[...] = jnp.zeros(yt_s.shape, F32)
            if has_h0:
                h_s[...] = h0_ref[:, ch, :]
            else:
                h_s[...] = jnp.zeros(h_s.shape, F32)

        def chunk_step(c, d):
            start = c * M_CHUNK
            rows = pl.ds(start if isinstance(c, int) else pl.multiple_of(start, M_CHUNK), M_CHUNK)
            keep = keeps[d]
            end = M_CHUNK - 1 if d == 0 else 0
            acum = acum_s[d, rows, :]
            a8 = a8_s[d, c]
            dt8 = dt8_s[d, c]
            a_end8 = jnp.broadcast_to(a8[:, end:end + 1], a8.shape)
            w8 = jnp.exp(a_end8 - a8) * dt8
            bc = b_s[rows, :]
            cc = c_s[rows, :]
            cb = _dot_nt(cc, bc)
            h_in = h_s[d]
            h_s[d] = (h_in * head_rows(jnp.exp(a_end8))
                      + _dot((xt_s[c] * head_rows(w8)).astype(BF16), bc))
            yt_s[c] += _dot_nt(h_in.astype(BF16), cc) * head_rows(jnp.exp(a8))
            for r in range(M_GROUP_HEADS):
                ln = M_GROUP_HEADS * d + r
                hs = slice(r * M_HEADDIM, (r + 1) * M_HEADDIM)
                seg = jnp.broadcast_to(acum[:, ln:ln + 1], (M_CHUNK, M_CHUNK)) - a8[r:r + 1, :]
                decay = jnp.exp(jnp.where(keep, seg, -jnp.inf))
                m = (cb * decay).astype(BF16)
                x_dt = (xt_s[c, hs, :] * dt8[r:r + 1, :]).astype(BF16)
                yt_s[c, hs, :] += _dot_nt(x_dt, m)

        def epilogue():
            for c in range(n_chunks):
                r = slice(c * M_CHUNK, (c + 1) * M_CHUNK)
                for k in range(M_GROUP_CH // LANES):
                    cs = slice(k * LANES, (k + 1) * LANES)
                    skip = dsk_ref[:, pl.ds(q * M_GROUP_CH + k * LANES, LANES)]
                    xs_s[r, cs] = xs_s[r, cs] * skip + yt_s[c, cs, :].T
            y = xs_s[...] * _silu(z_ref[:, ch])
            y = y * lax.rsqrt(jnp.mean(y * y, axis=-1, keepdims=True) + EPS) * ng_ref[:, ch]
            y_ref[:, ch] = y.astype(BF16)
            if emit_state:
                st_ref[:, ch, :] = h_s[...]

        return prologue, chunk_step, epilogue

    groups = [make_group(q) for q in range(gps)]
    for prologue, _, _ in groups:
        prologue()

    def both_directions(i):
        for _, chunk_step, _ in groups:
            chunk_step(i, 0)
        for _, chunk_step, _ in groups:
            chunk_step(n_chunks - 1 - i, 1)

    if n_chunks <= 2:
        for i in range(n_chunks):
            both_directions(i)
    else:
        def body(i, carry):
            both_directions(i)
            return carry
        lax.fori_loop(0, n_chunks, body, 0)

    for _, _, epilogue in groups:
        epilogue()


def _ssd(xbc, z, dt, conv_w, conv_b, dtb, alog, dsk, norm_g, h0, j, seq_len, emit_state):
    rows = xbc.shape[0]
    n_seq = rows // seq_len
    gps = min(M_GROUPS, SSD_ROWS_PER_STEP // seq_len)
    ch, st, hd = gps * M_GROUP_CH, gps * M_STATE, gps * LANES
    gx = M_INNER // st
    gc = gx + M_GROUPS * M_STATE // st
    has_h0 = h0 is not None
    n_chunks = seq_len // M_CHUNK
    st_block = (None, 2, ch, M_STATE)

    in_specs = [pl.BlockSpec((seq_len, ch), lambda s, g: (s, g)),
                pl.BlockSpec((seq_len, st), lambda s, g: (s, gx + g)),
                pl.BlockSpec((seq_len, st), lambda s, g: (s, gc + g)),
                pl.BlockSpec((seq_len, ch), lambda s, g: (s, g)),
                pl.BlockSpec((seq_len, hd), lambda s, g: (s, g)),
                pl.BlockSpec((None, M_CONV, ch), lambda s, g: (j, 0, g)),
                pl.BlockSpec((None, M_CONV, st), lambda s, g: (j, 0, gx + g)),
                pl.BlockSpec((None, M_CONV, st), lambda s, g: (j, 0, gc + g)),
                pl.BlockSpec((None, 1, ch), lambda s, g: (j, 0, g)),
                pl.BlockSpec((None, 1, st), lambda s, g: (j, 0, gx + g)),
                pl.BlockSpec((None, 1, st), lambda s, g: (j, 0, gc + g)),
                pl.BlockSpec((None, 1, hd), lambda s, g: (j, 0, g)),
                pl.BlockSpec((None, 1, hd), lambda s, g: (j, 0, g)),
                pl.BlockSpec((None, 1, ch), lambda s, g: (j, 0, g)),
                pl.BlockSpec((None, 1, ch), lambda s, g: (j, 0, g))]
    args = [xbc, xbc, xbc, z, dt, conv_w, conv_w, conv_w, conv_b, conv_b, conv_b,
            dtb, alog, dsk, norm_g]
    if has_h0:
        in_specs.append(pl.BlockSpec(st_block, lambda s, g: (s, 0, g, 0)))
        args.append(h0)
    out_specs = [pl.BlockSpec((seq_len, ch), lambda s, g: (s, g))]
    out_shape = [jax.ShapeDtypeStruct((rows, M_INNER), BF16)]
    if emit_state:
        out_specs.append(pl.BlockSpec(st_block, lambda s, g: (s, 0, g, 0)))
        out_shape.append(jax.ShapeDtypeStruct((n_seq, 2, M_INNER, M_STATE), F32))
    scratch = [pltpu.VMEM((gps, seq_len, M_GROUP_CH), F32),
               pltpu.VMEM((gps, n_chunks, M_GROUP_CH, M_CHUNK), F32),
               pltpu.VMEM((gps, seq_len, M_STATE), BF16),
               pltpu.VMEM((gps, seq_len, M_STATE), BF16),
               pltpu.VMEM((gps, 2, seq_len, LANES), F32),
               pltpu.VMEM((gps, 2, n_chunks, M_GROUP_HEADS, M_CHUNK), F32),
               pltpu.VMEM((gps, 2, n_chunks, M_GROUP_HEADS, M_CHUNK), F32),
               pltpu.VMEM((gps, n_chunks, M_GROUP_CH, M_CHUNK), F32),
               pltpu.VMEM((gps, 2, M_GROUP_CH, M_STATE), F32)]
    return pl.pallas_call(
        functools.partial(_ssd_kernel, seq_len=seq_len, has_h0=has_h0, emit_state=emit_state,
                          gps=gps),
        grid=(n_seq, M_GROUPS // gps),
        in_specs=in_specs,
        out_specs=out_specs,
        out_shape=out_shape,
        scratch_shapes=scratch,
        compiler_params=_cparams(2),
        name="ssd",
    )(*args)


def _gmlp_kernel(x_ref, mod_ref, g_ref, win_ref, bin_ref, ng_ref, ws_ref, bs_ref, wout_ref,
                 o_ref, gate_s, *, tm):
    x = x_ref[...]
    h = _modnorm(x, g_ref[1:2, :], mod_ref[3:4, :], mod_ref[4:5, :]).astype(BF16)
    hg = jax.nn.gelu(_dot(h, win_ref[...]) + bin_ref[...])
    u = hg[:, :G_INNER]
    v = hg[:, G_INNER:]
    v = v * lax.rsqrt(jnp.mean(v * v, axis=-1, keepdims=True) + EPS) * ng_ref[...]
    vb = v.astype(BF16)
    bs = bs_ref[...]
    for c in range(tm // G_CHUNK):
        r = slice(c * G_CHUNK, (c + 1) * G_CHUNK)
        for hd in range(G_HEADS):
            cs = slice(hd * G_HEAD_CH, (hd + 1) * G_HEAD_CH)
            sv = _dot(ws_ref[hd], vb[r, cs]) + bs[:, hd:hd + 1]
            gate_s[r, cs] = (u[r, cs] * sv).astype(BF16)
    o_ref[...] = x + mod_ref[5:6, :] * _dot(gate_s[...], wout_ref[...])


def _gmlp(x, mods, ln_g, w_in, b_in, norm_g, w_s, b_s_t, w_out, layer, j, pop):
    tm = 512
    rows = x.shape[0]
    return pl.pallas_call(
        functools.partial(_gmlp_kernel, tm=tm),
        grid=(rows // tm,),
        in_specs=[pl.BlockSpec((tm, D_MODEL), lambda t: (t, 0)),
                  _mod_spec(layer, pop, tm),
                  _ln_spec(layer),
                  _resident((D_MODEL, 2 * G_INNER), lambda t: (0, 0)),
                  _resident((None, 1, 2 * G_INNER), lambda t: (j, 0, 0)),
                  _resident((None, 1, G_INNER), lambda t: (j, 0, 0)),
                  _resident((None, G_HEADS, G_CHUNK, G_CHUNK), lambda t: (j, 0, 0, 0)),
                  _resident((None, G_CHUNK, G_HEADS), lambda t: (j, 0, 0)),
                  _resident((G_INNER, D_MODEL), lambda t: (0, 0))],
        out_specs=pl.BlockSpec((tm, D_MODEL), lambda t: (t, 0)),
        out_shape=jax.ShapeDtypeStruct((rows, D_MODEL), F32),
        scratch_shapes=[pltpu.VMEM((tm, G_INNER), BF16)],
        compiler_params=_cparams(1),
        name="gmlp",
    )(x, mods, ln_g, w_in, b_in, norm_g, w_s, b_s_t, w_out)


def _dft_tables(seq_len):
    def angles(n):
        k = np.arange(n, dtype=np.int64)
        return 2.0 * np.pi * ((k[:, None] * k[None, :]) % n).astype(np.float64) / n
    ac = angles(F_GROUP_CH)
    al = angles(seq_len)
    chan = np.concatenate([np.cos(ac), np.sin(ac)], axis=1).astype(np.float32)
    pos = np.concatenate([np.cos(al), -np.sin(al)], axis=1).astype(np.float32)
    return chan, pos


def _fnet_kernel(x_ref, mod_ref, g_ref, chan_ref, pos_ref, wout_ref, b_ref, o_ref, f_s,
                 *, seq_len):
    x = x_ref[...]
    h = _modnorm(x, g_ref[1:2, :], mod_ref[3:4, :], mod_ref[4:5, :]).astype(BF16)
    chan = chan_ref[...].astype(BF16)
    pos = pos_ref[...].astype(BF16)
    scale = 1.0 / math.sqrt(seq_len * F_GROUP_CH)
    for g in range(F_GROUPS):
        cs = slice(g * F_GROUP_CH, (g + 1) * F_GROUP_CH)
        p = _dot(h[:, cs], chan)
        for s in range(x.shape[0] // seq_len):
            r = slice(s * seq_len, (s + 1) * seq_len)
            stacked = jnp.concatenate([p[r, :F_GROUP_CH], p[r, F_GROUP_CH:]], axis=0)
            f_s[r, cs] = (_dot(pos, stacked.astype(BF16)) * scale).astype(BF16)
    o_ref[...] = x + mod_ref[5:6, :] * (_dot(f_s[...], wout_ref[...]) + b_ref[...])


def _fnet(x, mods, ln_g, w_out, b_out, layer, j, pop, seq_len):
    rows = x.shape[0]
    chan, pos = _dft_tables(seq_len)
    tm = max(seq_len, 1024)
    return pl.pallas_call(
        functools.partial(_fnet_kernel, seq_len=seq_len),
        grid=(rows // tm,),
        in_specs=[pl.BlockSpec((tm, D_MODEL), lambda t: (t, 0)),
                  _mod_spec(layer, pop, tm),
                  _ln_spec(layer),
                  _resident(chan.shape, lambda t: (0, 0)),
                  _resident(pos.shape, lambda t: (0, 0)),
                  _resident((D_MODEL, D_MODEL), lambda t: (0, 0)),
                  _resident((None, 1, D_MODEL), lambda t: (j, 0, 0))],
        out_specs=pl.BlockSpec((tm, D_MODEL), lambda t: (t, 0)),
        out_shape=jax.ShapeDtypeStruct((rows, D_MODEL), F32),
        scratch_shapes=[pltpu.VMEM((tm, D_MODEL), BF16)],
        compiler_params=_cparams(1),
        name="fnet",
    )(x, mods, ln_g, jnp.asarray(chan), jnp.asarray(pos), w_out, b_out)


def _head_mean_matrix():
    i = np.arange(2 * LANES)
    return ((i[:, None] // A_HD) == (i[None, :] // A_HD)).astype(np.float32) / A_HD


def _rope_tables():
    half = A_HD // 4
    inv = (np.float32(ROPE_THETA) ** (-np.arange(half, dtype=np.float32) / np.float32(half)))
    inv = inv.astype(np.float32)
    t = np.arange(DEC_SEQ)
    lane = np.arange(LANES)
    hl = lane % A_HD
    posn = np.where(hl[None, :] < A_HD // 2, (t // GRID_W)[:, None], (t % GRID_W)[:, None])
    sub = hl % (A_HD // 2)
    ang = posn.astype(np.float32) * inv[sub % half][None, :]
    cos = np.cos(ang).astype(np.float32)
    sin = np.sin(ang).astype(np.float32)
    lower = (sub < half)[None, :]
    sin_up = np.where(lower, -sin, 0.0).astype(np.float32)
    sin_dn = np.where(lower, 0.0, sin).astype(np.float32)
    return cos, sin_up, sin_dn


def _qkv_kernel(*refs, rope, cache_seq):
    it = iter(refs)
    x_ref, mod_ref, g_ref, w_ref, qg_ref, kg_ref, avg_ref = (next(it) for _ in range(7))
    if rope:
        cos_ref, sup_ref, sdn_ref = (next(it) for _ in range(3))
    q_ref, k_ref, v_ref = (next(it) for _ in range(3))
    if cache_seq:
        kt_ref, vt_ref = next(it), next(it)
    h = _modnorm(x_ref[...], g_ref[1:2, :], mod_ref[3:4, :], mod_ref[4:5, :]).astype(BF16)
    qkv = _dot(h, w_ref[...])
    avg = avg_ref[...]
    half = A_HD // 4

    def head_norm(t, gain):
        hi, lo = _split2(t * t)
        ms = _dot(hi, avg) + _dot(lo, avg)
        return t * lax.rsqrt(ms + EPS) * gain

    def rotate(t):
        up = pltpu.roll(t, LANES - half, 1)
        dn = pltpu.roll(t, half, 1)
        return t * cos_ref[...] + up * sup_ref[...] + dn * sdn_ref[...]

    slab = 2 * LANES
    for s in range((A_Q + A_KVW) // slab):
        cs = slice(s * slab, (s + 1) * slab)
        is_q = s < A_Q // slab
        gain = qg_ref[:, cs] if is_q else kg_ref[...]
        t = head_norm(qkv[:, cs], gain)
        if rope:
            t = jnp.concatenate([rotate(t[:, :LANES]), rotate(t[:, LANES:])], axis=1)
        if is_q:
            q_ref[:, cs] = t.astype(q_ref.dtype)
        else:
            k = t
    v = qkv[:, A_Q + A_KVW:]
    k_ref[...] = k.astype(k_ref.dtype)
    v_ref[...] = v.astype(v_ref.dtype)
    if cache_seq:
        for s in range(k.shape[0] // cache_seq):
            r = slice(s * cache_seq, (s + 1) * cache_seq)
            kt_ref[s] = k[r, :].T
            vt_ref[s] = v[r, :].T


def _qkv(x, mods, ln_g, w, q_gain, k_gain, layer, j, pop):
    tm = 512
    rows = x.shape[0]
    rope = pop == 1
    cache_seq = 0 if rope else SEQ
    in_specs = [pl.BlockSpec((tm, D_MODEL), lambda t: (t, 0)),
                _mod_spec(layer, pop, tm),
                _ln_spec(layer),
                _resident((D_MODEL, A_Q + 2 * A_KVW), lambda t: (0, 0)),
                _resident((None, 1, A_Q), lambda t: (j, 0, 0)),
                _resident((None, 1, A_KVW), lambda t: (j, 0, 0)),
                _resident((2 * LANES, 2 * LANES), lambda t: (0, 0))]
    args = [x, mods, ln_g, w, q_gain, k_gain, jnp.asarray(_head_mean_matrix(), BF16)]
    if rope:
        per_seq = DEC_SEQ // tm
        in_specs += [pl.BlockSpec((tm, LANES), lambda t: (t % per_seq, 0))] * 3
        args += [jnp.asarray(a) for a in _rope_tables()]
    out_specs = [pl.BlockSpec((tm, A_Q), lambda t: (t, 0)),
                 pl.BlockSpec((tm, A_KVW), lambda t: (t, 0)),
                 pl.BlockSpec((tm, A_KVW), lambda t: (t, 0))]
    out_shape = [jax.ShapeDtypeStruct((rows, A_Q), BF16),
                 jax.ShapeDtypeStruct((rows, A_KVW), BF16),
                 jax.ShapeDtypeStruct((rows, A_KVW), BF16)]
    if cache_seq:
        per_tile = tm // cache_seq
        out_specs += [pl.BlockSpec((per_tile, A_KVW, cache_seq), lambda t: (t, 0, 0))] * 2
        out_shape += [jax.ShapeDtypeStruct((rows // cache_seq, A_KVW, cache_seq), F32)] * 2
    return pl.pallas_call(
        functools.partial(_qkv_kernel, rope=rope, cache_seq=cache_seq),
        grid=(rows // tm,),
        in_specs=in_specs,
        out_specs=out_specs,
        out_shape=out_shape,
        compiler_params=_cparams(1),
        name="qkv",
    )(*args)


def _head_place_matrices():
    m = np.zeros((A_KV, A_KVW, A_GRP * A_HD), np.float32)
    d = np.arange(A_HD)
    for j in range(A_KV):
        for g in range(A_GRP):
            m[j, j * A_HD + d, g * A_HD + d] = 1.0
    return m


def _attn_kernel(q_ref, place_ref, *refs, n_src, n_batch):
    kv_refs, o_ref = refs[:2 * n_src], refs[2 * n_src]
    lane = lax.broadcasted_iota(jnp.int32, (1, A_GRP * A_HD), 1)
    scale = A_HD ** -0.5
    tq = q_ref.shape[0] // n_batch
    for j in range(A_KV):
        place = place_ref[j]
        cs = slice(j * A_GRP * A_HD, (j + 1) * A_GRP * A_HD)
        for b in range(n_batch):
            rq = slice(b * tq, (b + 1) * tq)
            qs = q_ref[rq, cs]
            keys = []
            for s in range(n_src):
                lk = kv_refs[2 * s].shape[0] // n_batch
                keys.append(slice(b * lk, (b + 1) * lk))
            k4 = [_dot(kv_refs[2 * s][keys[s], :].astype(BF16), place).astype(BF16)
                  for s in range(n_src)]
            v4 = [_dot(kv_refs[2 * s + 1][keys[s], :].astype(BF16), place).astype(BF16)
                  for s in range(n_src)]
            acc = None
            for g in range(A_GRP):
                own = (lane >= g * A_HD) & (lane < (g + 1) * A_HD)
                qm = jnp.where(own, qs, jnp.zeros_like(qs))
                sc = [_dot_nt(qm, k4[s]) * scale for s in range(n_src)]
                mx = functools.reduce(jnp.maximum,
                                      [jnp.max(t, axis=-1, keepdims=True) for t in sc])
                p = [jnp.exp(t - mx) for t in sc]
                den = functools.reduce(jnp.add, [jnp.sum(t, axis=-1, keepdims=True) for t in p])
                pv = functools.reduce(jnp.add, [
                    _dot(p[s].astype(BF16), jnp.where(own, v4[s], jnp.zeros_like(v4[s])))
                    for s in range(n_src)])
                acc = pv / den if acc is None else acc + pv / den
            o_ref[rq, cs] = acc.astype(o_ref.dtype)


def _attn(q, sources, seq_len, tq, batches_per_step=1):
    rows = q.shape[0]
    nb = batches_per_step
    per = seq_len // tq
    assert nb == 1 or per == 1
    in_specs = [pl.BlockSpec((nb * tq, A_Q), lambda b, t: (b * per + t, 0)),
                _resident((A_KV, A_KVW, A_GRP * A_HD), lambda b, t: (0, 0, 0))]
    args = [q, jnp.asarray(_head_place_matrices(), BF16)]
    for k, v, lk in sources:
        in_specs += [pl.BlockSpec((nb * lk, A_KVW), lambda b, t: (b, 0))] * 2
        args += [k, v]
    return pl.pallas_call(
        functools.partial(_attn_kernel, n_src=len(sources), n_batch=nb),
        grid=(rows // (seq_len * nb), per),
        in_specs=in_specs,
        out_specs=pl.BlockSpec((nb * tq, A_Q), lambda b, t: (b * per + t, 0)),
        out_shape=jax.ShapeDtypeStruct((rows, A_Q), BF16),
        compiler_params=_cparams(2),
        name="attn",
    )(*args)


def _regroup_heads(a):
    lead = a.shape[:-1]
    a = a.reshape(lead + (2, M_GROUPS, M_GROUP_HEADS))
    a = jnp.swapaxes(a, -3, -2).reshape(lead + (M_GROUPS, 2 * M_GROUP_HEADS))
    pad = [(0, 0)] * (len(lead) + 1) + [(0, LANES - 2 * M_GROUP_HEADS)]
    return jnp.pad(a, pad).reshape(lead + (M_GROUPS * LANES,))


def kernel(x_prompt, x_sample, state_ssm, cache_k, cache_v, c, c_ctx,
           ln_g, ada_w, ada_b, ff1_w_in, ff1_w_out, ff2_w_in, ff2_w_out,
           m_w_in, m_conv_w, m_conv_b, m_dt_bias, m_a_log, m_d, m_norm_g, m_w_out,
           g_w_in, g_b_in, g_norm_g, g_w_s, g_b_s, g_w_out,
           f_w_out, f_b_out,
           a_w_qkv, a_q_norm, a_k_norm, a_w_o):
    xs = [x_prompt.reshape(TP, D_MODEL), x_sample.reshape(TS, D_MODEL)]
    seq = (SEQ, DEC_SEQ)

    conds = jnp.concatenate(
        [c_ctx[None], c, jnp.zeros((COND_ROWS - 1 - DEC_BATCH, D_MODEL), F32)], axis=0)
    mods = _adaln(conds, ada_w, ada_b)

    ff_w = {(0, 0): (ff1_w_in[0].astype(BF16), ff1_w_out[0].astype(BF16))}
    n_zx = M_INNER + M_CONV_CH
    m_w_in_t = jnp.swapaxes(m_w_in, 1, 2)
    mixer_weights = ([(m_w_in_t, n_zx, D_MODEL), (m_w_out, M_INNER, D_MODEL)],
                     [(g_w_in, D_MODEL, 2 * G_INNER), (g_w_out, G_INNER, D_MODEL)],
                     [(f_w_out, D_MODEL, D_MODEL)],
                     [(a_w_qkv, D_MODEL, A_Q + 2 * A_KVW), (a_w_o, A_Q, D_MODEL)])

    new_ssm, new_k, new_v = [], [], []
    for i in range(DEPTH):
        kind, j = i % N_MIXERS, i // N_MIXERS
        later = [(i, 1)] + ([(i + 1, 0)] if i + 1 < DEPTH else [])
        convert = []
        for layer, second in later:
            pair = (ff2_w_in, ff2_w_out) if second else (ff1_w_in, ff1_w_out)
            convert += [(pair[0], layer, D_MODEL, 2 * FFN_HIDDEN),
                        (pair[1], layer, FFN_HIDDEN, D_MODEL)]
        convert += [(w, j, n_rows, n_cols) for w, n_rows, n_cols in mixer_weights[kind]]
        w_in, w_out = ff_w[(i, 0)]
        xs[0], cast = _ffn(xs[0], mods, ln_g, w_in, w_out, i, False, 0, convert=convert)
        for n, key in enumerate(later):
            ff_w[key] = (cast[2 * n], cast[2 * n + 1])
        mixer_w = cast[2 * len(later):]
        xs[1], _ = _ffn(xs[1], mods, ln_g, w_in, w_out, i, False, 1)
        mix = [None, None]
        if kind == 0:
            w_zx_t, w_out = mixer_w
            w_dt = _regroup_heads(m_w_in[j, :, n_zx:]).astype(BF16)
            dtb = _regroup_heads(m_dt_bias.reshape(-1, 1, 2 * M_HEADS))
            alog = _regroup_heads(m_a_log.reshape(-1, 1, 2 * M_HEADS))
            dsk = jnp.repeat(m_d, M_HEADDIM, axis=-1)[:, None, :]
            conv_b = m_conv_b[:, None, :]
            norm_g = m_norm_g[:, None, :]
            h0 = state_ssm[:, j].reshape(DEC_BATCH, 2, M_INNER, M_STATE)
            for p in range(2):
                z, xbc, dt = _modlinear(xs[p], mods, ln_g, [w_zx_t, w_dt], i, p,
                                        ((M_INNER, M_CONV_CH), (M_GROUPS * LANES,)),
                                        (True, False))
                res = _ssd(xbc, z, dt, m_conv_w, conv_b, dtb, alog, dsk, norm_g,
                           h0 if p == 1 else None, j, seq[p], emit_state=(p == 0))
                if p == 0:
                    new_ssm.append(res[1].reshape(BATCH, 2, M_HEADS, M_HEADDIM, M_STATE))
                mix[p] = (res[0], w_out)
        elif kind == 1:
            w_in, w_out = mixer_w
            w_s = g_w_s.astype(BF16)
            b_s_t = jnp.swapaxes(g_b_s, -1, -2)
            xs = [_gmlp(xs[p], mods, ln_g, w_in, g_b_in[:, None, :], g_norm_g[:, None, :],
                        w_s, b_s_t, w_out, i, j, p) for p in range(2)]
        elif kind == 2:
            w_out, = mixer_w
            xs = [_fnet(xs[p], mods, ln_g, w_out, f_b_out[:, None, :], i, j, p, seq[p])
                  for p in range(2)]
        else:
            w_qkv, w_o = mixer_w
            q_gain = jnp.tile(a_q_norm, (1, A_HEADS))[:, None, :]
            k_gain = jnp.tile(a_k_norm, (1, A_KV))[:, None, :]
            qp, kp, vp, kt, vt = _qkv(xs[0], mods, ln_g, w_qkv, q_gain, k_gain, i, j, 0)
            for cache, t in ((new_k, kt), (new_v, vt)):
                cache.append(jnp.transpose(t.reshape(BATCH, A_KV, A_HD, SEQ), (0, 3, 1, 2)))
            op = _attn(qp, [(kp, vp, SEQ)], SEQ, SEQ, batches_per_step=4)
            ql, kl, vl = _qkv(xs[1], mods, ln_g, w_qkv, q_gain, k_gain, i, j, 1)
            kc = cache_k[:, j].reshape(DEC_BATCH * PAST_LEN, A_KVW)
            vc = cache_v[:, j].reshape(DEC_BATCH * PAST_LEN, A_KVW)
            ol = _attn(ql, [(kc, vc, PAST_LEN), (kl, vl, DEC_SEQ)], DEC_SEQ, 256)
            mix = [(op, w_o), (ol, w_o)]
        w_in, w_out = ff_w[(i, 1)]
        xs = [_ffn(xs[p], mods, ln_g, w_in, w_out, i, True, p, mix[p])[0] for p in range(2)]

    return (xs[0].reshape(BATCH, SEQ, D_MODEL),
            xs[1].reshape(DEC_BATCH, DEC_SEQ, D_MODEL),
            jnp.stack(new_ssm, axis=1),
            jnp.stack(new_k, axis=1),
            jnp.stack(new_v, axis=1))
```
